```python
import math
import jax
import jax.numpy as jnp
from jax import lax
import numpy as np

D_MODEL = 1024
BATCH = 16
SEQ = 2048
DEPTH = 4

CTX_LEN = 256
GRID_W = 64
N_MIXERS = 2
N_ATTN_LAYERS = (DEPTH + 1) // 2
N_SSM_LAYERS = DEPTH // 2
DIFF_HEADS = 8
DIFF_HEAD_DIM = D_MODEL // DIFF_HEADS // 2
DIFF_V_DIM = 2 * DIFF_HEAD_DIM
Q_BLOCK = 128
ROPE_BASE = 10000.0
ROPE_F = DIFF_HEAD_DIM // 4
SSM_GROUP = 16
SSM_GROUPS = D_MODEL // SSM_GROUP
SSM_STATE = 64
SSM_CHUNK = 128
SSM_DIRS = 2
DT_MIN = 0.001
DT_MAX = 0.1
N_EXPERTS = 16
EXPERT_FF = 2 * D_MODEL
EC_CAPACITY_FACTOR = 2
N_MOD = 6
EPS = 1e-6

kernel_name = 'hybrid_diffattn_s5_ecmoe_dit'


def rmsnorm(x, g):
    xf = x.astype(jnp.float32)
    y = xf * lax.rsqrt(jnp.mean(xf * xf, axis=-1, keepdims=True) + EPS)
    return (y * g.astype(jnp.float32)).astype(x.dtype)


def modulate(x, shift, scale):
    return x * (1 + scale) + shift


def axial_rope_tables(rows):
    row = jnp.repeat(jnp.arange(rows, dtype=jnp.float32), GRID_W)
    col = jnp.tile(jnp.arange(GRID_W, dtype=jnp.float32), rows)
    inv = ROPE_BASE ** (-jnp.arange(ROPE_F, dtype=jnp.float32) / ROPE_F)
    ang = jnp.stack([row[:, None] * inv, col[:, None] * inv], axis=1)
    return jnp.cos(ang), jnp.sin(ang)


def apply_axial_rope(x, cos, sin):
    shp = x.shape
    xr = x.astype(jnp.float32).reshape(*shp[:-1], 2, 2, ROPE_F)
    x0, x1 = xr[..., 0, :], xr[..., 1, :]
    cs = cos[None, :, None, None]
    sn = sin[None, :, None, None]
    out = jnp.stack([x0 * cs - x1 * sn, x0 * sn + x1 * cs], axis=-2)
    return out.reshape(shp).astype(x.dtype)


def diff_attention(h_lat, h_ctx, w_qkv, w_o, lq1, lk1, lq2, lk2, subln_g, lam_init, cos, sin, with_ctx_out):
    B, L, _ = h_lat.shape
    H, d, dv = DIFF_HEADS, DIFF_HEAD_DIM, DIFF_V_DIM

    def project(h):
        n = h.shape[1]
        q, k, v = jnp.split(h @ w_qkv, 3, axis=-1)
        return q.reshape(B, n, H, 2, d), k.reshape(B, n, H, 2, d), v.reshape(B, n, H, dv)

    q_l, k_l, v_l = project(h_lat)
    q_c, k_c, v_c = project(h_ctx)
    q_l = apply_axial_rope(q_l, cos, sin)
    k_l = apply_axial_rope(k_l, cos, sin)
    f32 = jnp.float32
    lam = (jnp.exp(jnp.sum(lq1.astype(f32) * lk1.astype(f32)))
           - jnp.exp(jnp.sum(lq2.astype(f32) * lk2.astype(f32))) + lam_init)
    scale = d ** -0.5

    def diff_attend(q, k, v):
        s = jnp.einsum('bqhcd,bkhcd->bhcqk', q, k).astype(f32) * scale
        p = jax.nn.softmax(s, axis=-1)
        a = p[:, :, 0] - lam * p[:, :, 1]
        return jnp.einsum('bhqk,bkhe->bqhe', a.astype(v.dtype), v)

    k_all = jnp.concatenate([k_l, k_c], axis=1)
    v_all = jnp.concatenate([v_l, v_c], axis=1)
    nb = L // Q_BLOCK
    q_blocks = q_l.reshape(B, nb, Q_BLOCK, H, 2, d).transpose(1, 0, 2, 3, 4, 5)
    o_l = lax.map(lambda qb: diff_attend(qb, k_all, v_all), q_blocks)
    o_l = o_l.transpose(1, 0, 2, 3, 4).reshape(B, L, H, dv)

    def finish(o):
        o = rmsnorm(o, subln_g) * (1.0 - lam_init)
        return o.reshape(o.shape[0], o.shape[1], H * dv) @ w_o

    out_l = finish(o_l)
    out_c = finish(diff_attend(q_c, k_c, v_c)) if with_ctx_out else None
    return out_l, out_c


def s5_combine(e1, e2):
    a1r, a1i, b1r, b1i = e1
    a2r, a2i, b2r, b2i = e2
    return (a2r * a1r - a2i * a1i, a2r * a1i + a2i * a1r,
            a2r * b1r - a2i * b1i + b2r, a2r * b1i + a2i * b1r + b2i)


def s5_scan(u, lam_re, lam_im, log_dt, b_re, b_im, c_re, c_im, h_re, h_im):
    f32 = jnp.float32
    lam_re, lam_im = lam_re.astype(f32), lam_im.astype(f32)
    b_re, b_im = b_re.astype(f32), b_im.astype(f32)
    c_re, c_im = c_re.astype(f32), c_im.astype(f32)
    dt = jnp.exp(log_dt.astype(f32))[:, None]
    mag = jnp.exp(lam_re * dt)
    ang = lam_im * dt
    ab_re, ab_im = mag * jnp.cos(ang), mag * jnp.sin(ang)
    den = lam_re * lam_re + lam_im * lam_im
    nr, ni = ab_re - 1.0, ab_im
    coef_re = (nr * lam_re + ni * lam_im) / den
    coef_im = (ni * lam_re - nr * lam_im) / den
    bb_re = coef_re[..., None] * b_re - coef_im[..., None] * b_im
    bb_im = coef_re[..., None] * b_im + coef_im[..., None] * b_re
    B, T, G, P = u.shape
    nc = T // SSM_CHUNK
    u_chunks = u.reshape(B, nc, SSM_CHUNK, G, P).transpose(1, 2, 0, 3, 4)

    def chunk_step(carry, u_blk):
        s_re, s_im = carry
        bu_re = jnp.einsum('tbgp,gnp->tbgn', u_blk, bb_re)
        bu_im = jnp.einsum('tbgp,gnp->tbgn', u_blk, bb_im)
        bu_re = bu_re.at[0].add(ab_re * s_re - ab_im * s_im)
        bu_im = bu_im.at[0].add(ab_re * s_im + ab_im * s_re)
        a_re = jnp.broadcast_to(ab_re, bu_re.shape)
        a_im = jnp.broadcast_to(ab_im, bu_im.shape)
        _, _, x_re, x_im = lax.associative_scan(s5_combine, (a_re, a_im, bu_re, bu_im), axis=0)
        y = jnp.einsum('tbgn,gpn->tbgp', x_re, c_re) - jnp.einsum('tbgn,gpn->tbgp', x_im, c_im)
        return (x_re[-1], x_im[-1]), y

    (s_re, s_im), y = lax.scan(chunk_step, (h_re, h_im), u_chunks)
    y = y.transpose(2, 0, 1, 3, 4).reshape(B, T, G, P)
    return y, s_re, s_im


def s5_mixer(h_lat, h_ctx, lam_re, lam_im, log_dt, b_re, b_im, c_re, c_im, d_skip, w_glu1, w_glu2, with_ctx_out):
    B, L, Dm = h_lat.shape
    Lc = h_ctx.shape[1]
    G, P, N = SSM_GROUPS, SSM_GROUP, SSM_STATE
    f32 = jnp.float32
    u_l = h_lat.astype(f32).reshape(B, L, G, P)
    u_c = h_ctx.astype(f32).reshape(B, Lc, G, P)
    dsk = d_skip.astype(f32).reshape(G, P)
    y_l = dsk * u_l
    y_c = dsk * u_c if with_ctx_out else None
    zeros = jnp.zeros((B, G, N), f32)
    for direction in range(SSM_DIRS):
        prm = (lam_re[direction], lam_im[direction], log_dt[direction],
               b_re[direction], b_im[direction], c_re[direction], c_im[direction])
        uc = u_c if direction == 0 else u_c[:, ::-1]
        ul = u_l if direction == 0 else u_l[:, ::-1]
        yc, s_re, s_im = s5_scan(uc, *prm, zeros, zeros)
        yl, _, _ = s5_scan(ul, *prm, s_re, s_im)
        if direction == 1:
            yc, yl = yc[:, ::-1], yl[:, ::-1]
        y_l = y_l + yl
        if with_ctx_out:
            y_c = y_c + yc

    def glu(y, n):
        g = jax.nn.gelu(y.reshape(B, n, Dm)).astype(h_lat.dtype)
        return (g @ w_glu1) * jax.nn.sigmoid(g @ w_glu2)

    out_l = glu(y_l, L)
    out_c = glu(y_c, Lc) if with_ctx_out else None
    return out_l, out_c


def ec_moe(h, w_router, b_router, w_gate, w_up, w_down):
    B, n, Dm = h.shape
    cap = EC_CAPACITY_FACTOR * n // N_EXPERTS
    logits = (h @ w_router + b_router).astype(jnp.float32)
    affinity = jax.nn.softmax(logits, axis=-1)
    weight, idx = lax.top_k(jnp.swapaxes(affinity, 1, 2), cap)
    xin = jax.vmap(lambda hb, ib: hb[ib])(h, idx)
    hid = jax.nn.silu(jnp.einsum('becd,edf->becf', xin, w_gate)) * jnp.einsum('becd,edf->becf', xin, w_up)
    yout = jnp.einsum('becf,efd->becd', hid, w_down) * weight[..., None].astype(h.dtype)

    def scatter_one(ib, yb):
        return jnp.zeros((n, Dm), yb.dtype).at[ib.reshape(-1)].add(yb.reshape(-1, Dm))

    return jax.vmap(scatter_one)(idx, yout)


def setup_inputs(seed: int = 0) -> dict:
    key = jax.random.key(seed)
    ks = iter(jax.random.split(key, 40))

    def nrm(shape, s):
        return jax.random.normal(next(ks), shape, jnp.float32) * s

    D, E, F = D_MODEL, N_EXPERTS, EXPERT_FF
    G, P, N = SSM_GROUPS, SSM_GROUP, SSM_STATE
    NA, NS = N_ATTN_LAYERS, N_SSM_LAYERS
    x = nrm((BATCH, SEQ, D), 1.0)
    c = nrm((BATCH, D), 1.0)
    ctx = nrm((BATCH, CTX_LEN, D), 1.0)
    c_ctx = nrm((D,), 1.0)
    ada_w = nrm((DEPTH, D, N_MOD * D), 0.5 * D ** -0.5)
    ada_b = nrm((DEPTH, N_MOD * D), 0.02)
    norm1_g = 1.0 + nrm((DEPTH, D), 0.01)
    norm2_g = 1.0 + nrm((DEPTH, D), 0.01)
    final_g = 1.0 + nrm((D,), 0.01)
    attn_w_qkv = nrm((NA, D, 3 * D), D ** -0.5)
    attn_w_o = nrm((NA, D, D), D ** -0.5)
    attn_lam_q1 = nrm((NA, DIFF_HEAD_DIM), 0.1)
    attn_lam_k1 = nrm((NA, DIFF_HEAD_DIM), 0.1)
    attn_lam_q2 = nrm((NA, DIFF_HEAD_DIM), 0.1)
    attn_lam_k2 = nrm((NA, DIFF_HEAD_DIM), 0.1)
    attn_subln_g = 1.0 + nrm((NA, DIFF_V_DIM), 0.01)
    ssm_lam_re = -0.5 + nrm((NS, SSM_DIRS, G, N), 0.01)
    ssm_lam_im = math.pi * jnp.arange(N, dtype=jnp.float32) + nrm((NS, SSM_DIRS, G, N), 0.01)
    ssm_log_dt = jax.random.uniform(next(ks), (NS, SSM_DIRS, G), jnp.float32,
                                    math.log(DT_MIN), math.log(DT_MAX))
    ssm_b_re = nrm((NS, SSM_DIRS, G, N, P), (2 * P) ** -0.5)
    ssm_b_im = nrm((NS, SSM_DIRS, G, N, P), (2 * P) ** -0.5)
    ssm_c_re = nrm((NS, SSM_DIRS, G, P, N), (2 * N) ** -0.5)
    ssm_c_im = nrm((NS, SSM_DIRS, G, P, N), (2 * N) ** -0.5)
    ssm_d = 1.0 + nrm((NS, D), 0.1)
    ssm_w_glu1 = nrm((NS, D, D), D ** -0.5)
    ssm_w_glu2 = nrm((NS, D, D), D ** -0.5)
    moe_w_router = nrm((DEPTH, D, E), D ** -0.5)
    moe_b_router = nrm((DEPTH, E), 0.01)
    moe_w_gate = nrm((DEPTH, E, D, F), D ** -0.5)
    moe_w_up = nrm((DEPTH, E, D, F), D ** -0.5)
    moe_w_down = nrm((DEPTH, E, F, D), F ** -0.5)
    return {'x': x, 'c': c, 'ctx': ctx, 'c_ctx': c_ctx,
            'ada_w': ada_w, 'ada_b': ada_b, 'norm1_g': norm1_g, 'norm2_g': norm2_g, 'final_g': final_g,
            'attn_w_qkv': attn_w_qkv, 'attn_w_o': attn_w_o,
            'attn_lam_q1': attn_lam_q1, 'attn_lam_k1': attn_lam_k1,
            'attn_lam_q2': attn_lam_q2, 'attn_lam_k2': attn_lam_k2, 'attn_subln_g': attn_subln_g,
            'ssm_lam_re': ssm_lam_re, 'ssm_lam_im': ssm_lam_im, 'ssm_log_dt': ssm_log_dt,
            'ssm_b_re': ssm_b_re, 'ssm_b_im': ssm_b_im, 'ssm_c_re': ssm_c_re, 'ssm_c_im': ssm_c_im,
            'ssm_d': ssm_d, 'ssm_w_glu1': ssm_w_glu1, 'ssm_w_glu2': ssm_w_glu2,
            'moe_w_router': moe_w_router, 'moe_b_router': moe_b_router,
            'moe_w_gate': moe_w_gate, 'moe_w_up': moe_w_up, 'moe_w_down': moe_w_down}


def reference(x, c, ctx, c_ctx, ada_w, ada_b, norm1_g, norm2_g, final_g,
              attn_w_qkv, attn_w_o, attn_lam_q1, attn_lam_k1, attn_lam_q2, attn_lam_k2, attn_subln_g,
              ssm_lam_re, ssm_lam_im, ssm_log_dt, ssm_b_re, ssm_b_im, ssm_c_re, ssm_c_im,
              ssm_d, ssm_w_glu1, ssm_w_glu2,
              moe_w_router, moe_b_router, moe_w_gate, moe_w_up, moe_w_down):
    B, L, Dm = x.shape
    rows = L // GRID_W
    cos, sin = axial_rope_tables(rows)
    sc_lat = jax.nn.silu(c)
    sc_ctx = jax.nn.silu(c_ctx)
    for i in range(DEPTH):
        with_ctx_out = i < DEPTH - 1
        j = i // N_MIXERS
        mod_l = (sc_lat @ ada_w[i] + ada_b[i])[:, None, :]
        mod_c = sc_ctx @ ada_w[i] + ada_b[i]
        sh1_l, sc1_l, g1_l, sh2_l, sc2_l, g2_l = jnp.split(mod_l, N_MOD, axis=-1)
        sh1_c, sc1_c, g1_c, sh2_c, sc2_c, g2_c = jnp.split(mod_c, N_MOD, axis=-1)
        h_l = modulate(rmsnorm(x, norm1_g[i]), sh1_l, sc1_l)
        h_c = modulate(rmsnorm(ctx, norm1_g[i]), sh1_c, sc1_c)
        if i % N_MIXERS == 0:
            lam_init = 0.8 - 0.6 * math.exp(-0.3 * i)
            o_l, o_c = diff_attention(h_l, h_c, attn_w_qkv[j], attn_w_o[j], attn_lam_q1[j], attn_lam_k1[j],
                                      attn_lam_q2[j], attn_lam_k2[j], attn_subln_g[j], lam_init,
                                      cos, sin, with_ctx_out)
        else:
            o_l, o_c = s5_mixer(h_l, h_c, ssm_lam_re[j], ssm_lam_im[j], ssm_log_dt[j], ssm_b_re[j], ssm_b_im[j],
                                ssm_c_re[j], ssm_c_im[j], ssm_d[j], ssm_w_glu1[j], ssm_w_glu2[j], with_ctx_out)
        x = x + g1_l * o_l
        h_l = modulate(rmsnorm(x, norm2_g[i]), sh2_l, sc2_l)
        x = x + g2_l * ec_moe(h_l, moe_w_router[i], moe_b_router[i], moe_w_gate[i], moe_w_up[i], moe_w_down[i])
        if with_ctx_out:
            ctx = ctx + g1_c * o_c
            h_c = modulate(rmsnorm(ctx, norm2_g[i]), sh2_c, sc2_c)
            ctx = ctx + g2_c * ec_moe(h_c, moe_w_router[i], moe_b_router[i], moe_w_gate[i], moe_w_up[i], moe_w_down[i])
    return rmsnorm(x, final_g)
```

```python
import functools
import math

import jax
import jax.numpy as jnp
from jax import lax
from jax.experimental import pallas as pl
from jax.experimental.pallas import tpu as pltpu

F32 = jnp.float32
BF16 = jnp.bfloat16
I32 = jnp.int32

EPS = 1e-6
N_MOD = 6
HEADS = 8
HEAD_DIM = 64
V_DIM = 2 * HEAD_DIM
ROPE_BASE = 10000.0
ROPE_F = HEAD_DIM // 4
GRID_W = 64
S5_CHUNK = 32
LANES = 128
VMEM_LIMIT = 56 * 1024 * 1024


def _cparams(*sem):
    return pltpu.CompilerParams(dimension_semantics=sem, vmem_limit_bytes=VMEM_LIMIT)


def _split(a):
    hi = a.astype(BF16)
    lo = (a - hi.astype(F32)).astype(BF16)
    return hi, lo


def _dot3(a, b, dims):
    ah, al = _split(a)
    bh, bl = _split(b)
    dn = (dims, ((), ()))
    d = functools.partial(lax.dot_general, dimension_numbers=dn, preferred_element_type=F32)
    return d(ah, bh) + (d(ah, bl) + d(al, bh))


def _rms_mod(x, g, shift, scale):
    ms = jnp.mean(x * x, axis=-1, keepdims=True)
    return (x * lax.rsqrt(ms + EPS) * g) * (1.0 + scale) + shift


def _mod_kernel(c_ref, w_ref, b_ref, o_ref):
    c = c_ref[...]
    s = c * jax.nn.sigmoid(c)
    o_ref[...] = _dot3(s, w_ref[...], ((1,), (0,))) + b_ref[...]


def _adaln(cc, ada_w, ada_b):
    depth, d, n = ada_w.shape
    r = cc.shape[0]
    tn = n // 4
    return pl.pallas_call(
        _mod_kernel,
        grid=(depth, n // tn),
        in_specs=[pl.BlockSpec((r, d), lambda i, j: (0, 0)),
                  pl.BlockSpec((None, d, tn), lambda i, j: (i, 0, j)),
                  pl.BlockSpec((None, 1, tn), lambda i, j: (i, 0, j))],
        out_specs=pl.BlockSpec((None, r, tn), lambda i, j: (i, 0, j)),
        out_shape=jax.ShapeDtypeStruct((depth, r, n), F32),
        compiler_params=_cparams("parallel", "parallel"),
        name="adaln_mod",
    )(cc, ada_w, ada_b.reshape(depth, 1, n))


def _row_specs(tm, d, nct):
    x_spec = pl.BlockSpec((None, tm, d), lambda b, r: (b, r, 0))
    mt_spec = pl.BlockSpec((None, None, 8, d), lambda b, r: (b, jnp.where(r < nct, 0, 1), 0, 0))
    return x_spec, mt_spec


def _full(shape):
    return pl.BlockSpec(shape, lambda b, r: (0,) * len(shape))


def _qkv_kernel(x_ref, mt_ref, g_ref, w_ref, cos_ref, sin_ref, o_ref, *, d, tn):
    h = _rms_mod(x_ref[...], g_ref[...], mt_ref[0:1, :], mt_ref[1:2, :]).astype(BF16)
    reps = tn // LANES
    cosw = jnp.concatenate([cos_ref[...]] * reps, axis=1)
    sinw = jnp.concatenate([sin_ref[...]] * reps, axis=1)
    lane = lax.broadcasted_iota(I32, (1, tn), 1)
    first_half = (lane % (2 * ROPE_F)) < ROPE_F
    for j in range(3 * d // tn):
        acc = jnp.dot(h, w_ref[:, j * tn:(j + 1) * tn], preferred_element_type=F32)
        if j * tn < 2 * d:
            partner = jnp.where(first_half, pltpu.roll(acc, tn - ROPE_F, 1), pltpu.roll(acc, ROPE_F, 1))
            acc = acc * cosw + partner * sinw
        if j * tn < d:
            acc = acc * (HEAD_DIM ** -0.5)
        o_ref[:, j * tn:(j + 1) * tn] = acc.astype(BF16)


def _qkv_proj(xs, mt, g, w, cos_t, sin_t, tm, nct):
    b, t, d = xs.shape
    tn = 512
    x_spec, mt_spec = _row_specs(tm, d, nct)
    return pl.pallas_call(
        functools.partial(_qkv_kernel, d=d, tn=tn),
        grid=(b, t // tm),
        in_specs=[x_spec, mt_spec, _full((1, d)), _full((d, 3 * d)),
                  pl.BlockSpec((tm, LANES), lambda b, r: (r, 0)),
                  pl.BlockSpec((tm, LANES), lambda b, r: (r, 0))],
        out_specs=pl.BlockSpec((None, tm, 3 * d), lambda b, r: (b, r, 0)),
        out_shape=jax.ShapeDtypeStruct((b, t, 3 * d), BF16),
        compiler_params=_cparams("parallel", "parallel"),
        name="norm_qkv_rope",
    )(xs, mt, g, w, cos_t, sin_t)


def _attn_kernel(q_ref, k_ref, v_ref, lam_ref, g_ref, o_ref, *, lc, tq, lam_init):
    lv = lam_ref[...]
    e1 = jnp.exp(jnp.sum(lv[0:1] * lv[1:2], axis=1, keepdims=True))
    e2 = jnp.exp(jnp.sum(lv[2:3] * lv[3:4], axis=1, keepdims=True))
    lam = e1 - e2 + lam_init
    g = g_ref[...] * (1.0 - lam_init)
    lane = lax.broadcasted_iota(I32, (1, V_DIM), 1)
    comp = (lane < HEAD_DIM, lane >= HEAD_DIM)
    nt = (((1,), (1,)), ((), ()))

    def block(row0, nrows, kk, vv):
        q = q_ref[pl.ds(row0, nrows), :]
        outs = []
        for c in range(2):
            qc = jnp.where(comp[c], q, jnp.zeros_like(q))
            s = lax.dot_general(qc, kk, nt, preferred_element_type=F32)
            p = jnp.exp(s - jnp.max(s, axis=-1, keepdims=True))
            den = jnp.sum(p, axis=-1, keepdims=True)
            outs.append(jnp.dot(p.astype(BF16), vv, preferred_element_type=F32) / den)
        o = outs[0] - lam * outs[1]
        o = o * lax.rsqrt(jnp.mean(o * o, axis=-1, keepdims=True) + EPS) * g
        o_ref[pl.ds(row0, nrows), :] = o.astype(BF16)

    tc = min(tq, lc)
    for r in range(lc // tc):
        block(r * tc, tc, k_ref[0:lc, :], v_ref[0:lc, :])

    kk = k_ref[...]
    vv = v_ref[...]
    n_lat = (q_ref.shape[0] - lc) // tq

    def body(i, carry):
        block(pl.multiple_of(lc + i * tq, tq), tq, kk, vv)
        return carry

    lax.fori_loop(0, n_lat, body, 0)


def _diff_attention(qkv, lam_vecs, subln_g, lc, lam_init):
    b, t, d3 = qkv.shape
    d = d3 // 3
    tq = 256 if (t - lc) % 256 == 0 and lc % 256 == 0 else 128
    slab = lambda off: pl.BlockSpec((None, t, V_DIM), lambda b, h: (b, 0, off + h))
    return pl.pallas_call(
        functools.partial(_attn_kernel, lc=lc, tq=tq, lam_init=lam_init),
        grid=(b, HEADS),
        in_specs=[slab(0), slab(HEADS), slab(2 * HEADS),
                  pl.BlockSpec((8, LANES), lambda b, h: (0, 0)),
                  pl.BlockSpec((1, V_DIM), lambda b, h: (0, 0))],
        out_specs=pl.BlockSpec((None, t, V_DIM), lambda b, h: (b, 0, h)),
        out_shape=jax.ShapeDtypeStruct((b, t, d), BF16),
        compiler_params=_cparams("parallel", "parallel"),
        name="diff_attention",
    )(qkv, qkv, qkv, lam_vecs, subln_g)


def _proj_res_kernel(a_ref, w_ref, x_ref, mt_ref, o_ref):
    acc = jnp.dot(a_ref[...], w_ref[...], preferred_element_type=F32)
    o_ref[...] = x_ref[...] + mt_ref[2:3, :] * acc


def _proj_res(a, w, xs, mt, tm, nct):
    b, t, d = xs.shape
    x_spec, mt_spec = _row_specs(tm, d, nct)
    return pl.pallas_call(
        _proj_res_kernel,
        grid=(b, t // tm),
        in_specs=[x_spec, _full((d, d)), x_spec, mt_spec],
        out_specs=x_spec,
        out_shape=jax.ShapeDtypeStruct((b, t, d), F32),
        compiler_params=_cparams("parallel", "parallel"),
        name="out_proj_residual",
    )(a, w, xs, mt)


def _glu_res_kernel(a_ref, w1_ref, w2_ref, x_ref, mt_ref, o_ref):
    a = a_ref[...]
    z1 = jnp.dot(a, w1_ref[...], preferred_element_type=F32)
    z2 = jnp.dot(a, w2_ref[...], preferred_element_type=F32)
    o_ref[...] = x_ref[...] + mt_ref[2:3, :] * (z1 * jax.nn.sigmoid(z2))


def _glu_res(a, w1, w2, xs, mt, tm, nct):
    b, t, d = xs.shape
    x_spec, mt_spec = _row_specs(tm, d, nct)
    return pl.pallas_call(
        _glu_res_kernel,
        grid=(b, t // tm),
        in_specs=[x_spec, _full((d, d)), _full((d, d)), x_spec, mt_spec],
        out_specs=x_spec,
        out_shape=jax.ShapeDtypeStruct((b, t, d), F32),
        compiler_params=_cparams("parallel", "parallel"),
        name="glu_residual",
    )(a, w1, w2, xs, mt)


def _norm_mod_kernel(x_ref, mt_ref, g_ref, o_ref):
    o_ref[...] = _rms_mod(x_ref[...], g_ref[...], mt_ref[0:1, :], mt_ref[1:2, :])


def _norm_mod(xs, mt, g, tm, nct):
    b, t, d = xs.shape
    x_spec, mt_spec = _row_specs(tm, d, nct)
    return pl.pallas_call(
        _norm_mod_kernel,
        grid=(b, t // tm),
        in_specs=[x_spec, mt_spec, _full((1, d))],
        out_specs=x_spec,
        out_shape=jax.ShapeDtypeStruct((b, t, d), F32),
        compiler_params=_cparams("parallel", "parallel"),
        name="norm_modulate",
    )(xs, mt, g)


def _s5_in_kernel(u_ref, m_ref, o_ref):
    o_ref[...] = jnp.dot(u_ref[...].astype(BF16), m_ref[...], preferred_element_type=F32)


def _s5_scan_kernel(in_ref, a_ref, s_ref, *, nb, nc, ncc):
    a1 = a_ref[0:1, :]
    a2 = a_ref[1:2, :]
    half = LANES // 2

    def step(s, inc, lo):
        sl = slice(lo, lo + LANES)
        return a1[:, sl] * s + a2[:, sl] * pltpu.roll(s, half, 1) + inc

    def body(i, carry):
        s0, s1 = carry
        r0 = pl.multiple_of(i * nb, nb)
        c1 = jnp.where(i < ncc, ncc - 1 - i, nc + ncc - 1 - i)
        r1 = pl.multiple_of(c1 * nb, nb)
        s_ref[pl.ds(r0, nb), 0:LANES] = s0
        s_ref[pl.ds(r1, nb), LANES:2 * LANES] = s1
        s0 = step(s0, in_ref[pl.ds(r0, nb), 0:LANES], 0)
        s1 = step(s1, in_ref[pl.ds(r1, nb), LANES:2 * LANES], LANES)
        return s0, s1

    z = jnp.zeros((nb, LANES), F32)
    lax.fori_loop(0, nc, body, (z, z))


def _s5_out_kernel(u_ref, s_ref, mi_ref, mo_ref, d_ref, o_ref):
    u = u_ref[...]
    y = jnp.dot(u.astype(BF16), mi_ref[...], preferred_element_type=F32)
    y = y + jnp.dot(s_ref[...].astype(BF16), mo_ref[...], preferred_element_type=F32)
    y = y + d_ref[...] * u
    o_ref[...] = jax.nn.gelu(y).astype(BF16)


def _s5_matrices(lam_re, lam_im, log_dt, b_re, b_im, c_re, c_im, d_skip):
    tc = S5_CHUNK
    hp = lax.Precision.HIGHEST
    ndir, g, n = lam_re.shape
    p = b_re.shape[-1]
    dt = jnp.exp(log_dt)[..., None]
    mag = jnp.exp(lam_re * dt)
    ang = lam_im * dt
    ab_re, ab_im = mag * jnp.cos(ang), mag * jnp.sin(ang)
    den = lam_re * lam_re + lam_im * lam_im
    nr, ni = ab_re - 1.0, ab_im
    coef_re = (nr * lam_re + ni * lam_im) / den
    coef_im = (ni * lam_re - nr * lam_im) / den
    bb_re = coef_re[..., None] * b_re - coef_im[..., None] * b_im
    bb_im = coef_re[..., None] * b_im + coef_im[..., None] * b_re
    k = jnp.arange(tc + 1, dtype=F32)[:, None, None, None]
    pw_mag = jnp.exp(k * (lam_re * dt)[None])
    pw_re = pw_mag * jnp.cos(k * ang[None])
    pw_im = pw_mag * jnp.sin(k * ang[None])
    cp_re = jnp.einsum('dgpn,kdgn->dgkpn', c_re, pw_re) - jnp.einsum('dgpn,kdgn->dgkpn', c_im, pw_im)
    cp_im = jnp.einsum('dgpn,kdgn->dgkpn', c_re, pw_im) + jnp.einsum('dgpn,kdgn->dgkpn', c_im, pw_re)
    kern = (jnp.einsum('dgkpn,dgnq->dgkpq', cp_re, bb_re, precision=hp)
            - jnp.einsum('dgkpn,dgnq->dgkpq', cp_im, bb_im, precision=hp))
    s_idx = jnp.arange(tc)[:, None]
    t_idx = jnp.arange(tc)[None, :]
    lag_f = jnp.clip(t_idx - s_idx, 0, tc)
    lag_b = jnp.clip(s_idx - t_idx, 0, tc)
    m_f = jnp.where((t_idx >= s_idx)[None, :, :, None, None], kern[0][:, lag_f], 0.0)
    m_b = jnp.where((s_idx >= t_idx)[None, :, :, None, None], kern[1][:, lag_b], 0.0)
    m_intra = (m_f + m_b).transpose(0, 1, 4, 2, 3).reshape(g, tc * p, tc * p)
    cf_re, cf_im = cp_re[0][:, 1:], cp_im[0][:, 1:]
    cb_re, cb_im = cp_re[1][:, tc:0:-1], cp_im[1][:, tc:0:-1]
    to_rows = lambda a: a.transpose(0, 3, 1, 2).reshape(g, n, tc * p)
    m_out = jnp.concatenate([to_rows(cf_re), to_rows(-cf_im), to_rows(cb_re), to_rows(-cb_im)], axis=1)
    def in_cols(pr, pi_, br, bi):
        re = jnp.einsum('sgn,gnp->gspn', pr, br) - jnp.einsum('sgn,gnp->gspn', pi_, bi)
        im = jnp.einsum('sgn,gnp->gspn', pr, bi) + jnp.einsum('sgn,gnp->gspn', pi_, br)
        return re.reshape(g, tc * p, n), im.reshape(g, tc * p, n)
    f_re, f_im = in_cols(pw_re[tc - 1::-1, 0][:tc], pw_im[tc - 1::-1, 0][:tc], bb_re[0], bb_im[0])
    r_re, r_im = in_cols(pw_re[:tc, 1], pw_im[:tc, 1], bb_re[1], bb_im[1])
    m_in = jnp.concatenate([f_re, f_im, r_re, r_im], axis=2)
    a_re, a_im = pw_re[tc], pw_im[tc]
    a1 = jnp.concatenate([a_re[0], a_re[0], a_re[1], a_re[1]], axis=-1)
    a2 = jnp.concatenate([-a_im[0], a_im[0], -a_im[1], a_im[1]], axis=-1)
    decay = jnp.stack([a1, a2], axis=1)
    dsk = jnp.tile(d_skip.reshape(g, 1, p), (1, 1, tc))
    return m_intra.astype(BF16), m_out.astype(BF16), m_in.astype(BF16), decay, dsk


def _s5_mixer(h, mats, lc):
    m_intra, m_out, m_in, decay, dsk = mats
    b, t, d = h.shape
    g = m_intra.shape[0]
    p = d // g
    tc = S5_CHUNK
    nc, ncc = t // tc, lc // tc
    rows, kw = nc * b, tc * p
    sw = m_in.shape[2]
    u = h.reshape(b, nc, tc, g, p).transpose(3, 1, 0, 2, 4).reshape(g, rows, kw)
    per_g = lambda shape: pl.BlockSpec((None,) + shape, lambda i: (i, 0, 0))
    s_in = pl.pallas_call(
        _s5_in_kernel, grid=(g,),
        in_specs=[per_g((rows, kw)), per_g((kw, sw))],
        out_specs=per_g((rows, sw)),
        out_shape=jax.ShapeDtypeStruct((g, rows, sw), F32),
        compiler_params=_cparams("parallel"), name="s5_chunk_input",
    )(u, m_in)
    states = pl.pallas_call(
        functools.partial(_s5_scan_kernel, nb=b, nc=nc, ncc=ncc), grid=(g,),
        in_specs=[per_g((rows, sw)), per_g((2, sw))],
        out_specs=per_g((rows, sw)),
        out_shape=jax.ShapeDtypeStruct((g, rows, sw), F32),
        compiler_params=_cparams("parallel"), name="s5_chunk_scan",
    )(s_in, decay)
    y = pl.pallas_call(
        _s5_out_kernel, grid=(g,),
        in_specs=[per_g((rows, kw)), per_g((rows, sw)), per_g((kw, kw)), per_g((sw, kw)), per_g((1, kw))],
        out_specs=per_g((rows, kw)),
        out_shape=jax.ShapeDtypeStruct((g, rows, kw), BF16),
        compiler_params=_cparams("parallel"), name="s5_chunk_output",
    )(u, states, m_intra, m_out, dsk)
    return y.reshape(g, nc, b, tc, p).transpose(2, 1, 3, 0, 4).reshape(b, t, d)


def _router_kernel(x_ref, mt_ref, g_ref, wr_ref, br_ref, h_ref, aff_ref):
    h = _rms_mod(x_ref[...], g_ref[...], mt_ref[3:4, :], mt_ref[4:5, :])
    h_ref[...] = h.astype(BF16)
    logits = _dot3(wr_ref[...], h, ((1,), (1,))) + br_ref[...]
    p = jnp.exp(logits - jnp.max(logits, axis=0, keepdims=True))
    aff_ref[...] = p / jnp.sum(p, axis=0, keepdims=True)


def _router(xs, mt, g, w_rt, b_r, tm, nct):
    b, t, d = xs.shape
    e = w_rt.shape[0]
    x_spec, mt_spec = _row_specs(tm, d, nct)
    return pl.pallas_call(
        _router_kernel,
        grid=(b, t // tm),
        in_specs=[x_spec, mt_spec, _full((1, d)), _full((e, d)), _full((e, 1))],
        out_specs=[x_spec, pl.BlockSpec((None, e, tm), lambda b, r: (b, 0, r))],
        out_shape=[jax.ShapeDtypeStruct((b, t, d), BF16), jax.ShapeDtypeStruct((b, e, t), F32)],
        compiler_params=_cparams("parallel", "parallel"),
        name="norm_router",
    )(xs, mt, g, w_rt, b_r)


def _select_slots(aff, cap, tri):
    e, n = aff.shape
    bits = pltpu.bitcast(aff, I32)
    capf = float(cap)

    def body(_, carry):
        lo, hi = carry
        mid = lo + ((hi - lo) >> 1)
        cnt = jnp.sum(jnp.where(bits >= mid, 1.0, 0.0), axis=1, keepdims=True)
        ge = cnt >= capf
        return jnp.where(ge, mid, lo), jnp.where(ge, hi, mid)

    lo0 = jnp.zeros((e, 1), I32)
    hi0 = jnp.full((e, 1), 0x7F800001, I32)
    thr, _ = lax.fori_loop(0, 31, body, (lo0, hi0))
    gt = bits > thr
    eq = bits == thr
    need = capf - jnp.sum(jnp.where(gt, 1.0, 0.0), axis=1, keepdims=True)
    eq_rank = jnp.dot(jnp.where(eq, 1.0, 0.0).astype(BF16), tri, preferred_element_type=F32)
    sel = jnp.where(gt, 1.0, jnp.where(eq, jnp.where(eq_rank < need, 1.0, 0.0), 0.0))
    rank = jnp.dot(sel.astype(BF16), tri, preferred_element_type=F32)
    return jnp.where(sel > 0.5, rank.astype(I32), -1)


def _topk_kernel(aff_ref, tri_ref, slot_ref, *, lc, cap_c, cap_l):
    slot_ref[:, 0:lc] = _select_slots(aff_ref[:, 0:lc], cap_c, tri_ref[0:lc, 0:lc])
    slot_ref[:, lc:] = _select_slots(aff_ref[:, lc:], cap_l, tri_ref[...])


def _topk(aff, tri, lc, cap_c, cap_l):
    b, e, t = aff.shape
    l = t - lc
    return pl.pallas_call(
        functools.partial(_topk_kernel, lc=lc, cap_c=cap_c, cap_l=cap_l),
        grid=(b,),
        in_specs=[pl.BlockSpec((None, e, t), lambda i: (i, 0, 0)), pl.BlockSpec((l, l), lambda i: (0, 0))],
        out_specs=pl.BlockSpec((None, e, t), lambda i: (i, 0, 0)),
        out_shape=jax.ShapeDtypeStruct((b, e, t), I32),
        compiler_params=_cparams("parallel"),
        name="expert_choice_topk",
    )(aff, tri)


def _expert_kernel(h_ref, slot_ref, aff_ref, wg_ref, wu_ref, wd_ref, yl_ref, yc_ref, *, lc, cap_c, cap_l):
    slot = slot_ref[...]
    aff = aff_ref[...]

    def gather(lo, hi, cap):
        n = hi - lo
        hit = lax.broadcasted_iota(I32, (cap, n), 0) == slot[:, lo:hi]
        onehot = jnp.where(hit, 1.0, 0.0).astype(BF16)
        xin = jnp.dot(onehot, h_ref[lo:hi, :], preferred_element_type=F32).astype(BF16)
        w = jnp.sum(jnp.where(hit, aff[:, lo:hi], 0.0), axis=1, keepdims=True)
        return xin, w

    xl, wl = gather(lc, h_ref.shape[0], cap_l)
    xc, wc = gather(0, lc, cap_c)
    xin = jnp.concatenate([xl, xc], axis=0)
    gate = jnp.dot(xin, wg_ref[...], preferred_element_type=F32)
    up = jnp.dot(xin, wu_ref[...], preferred_element_type=F32)
    hid = (gate * jax.nn.sigmoid(gate) * up).astype(BF16)
    y = jnp.dot(hid, wd_ref[...], preferred_element_type=F32)
    yl_ref[...] = (y[0:cap_l] * wl).astype(BF16)
    yc_ref[...] = (y[cap_l:] * wc).astype(BF16)


def _experts(hb, slot4, aff4, wg, wu, wd, lc, cap_c, cap_l):
    b, t, d = hb.shape
    e, _, f = wg.shape
    row = pl.BlockSpec((None, None, 1, t), lambda e_, b_: (b_, e_, 0, 0))
    return pl.pallas_call(
        functools.partial(_expert_kernel, lc=lc, cap_c=cap_c, cap_l=cap_l),
        grid=(e, b),
        in_specs=[pl.BlockSpec((None, t, d), lambda e_, b_: (b_, 0, 0)), row, row,
                  pl.BlockSpec((None, d, f), lambda e_, b_: (e_, 0, 0)),
                  pl.BlockSpec((None, d, f), lambda e_, b_: (e_, 0, 0)),
                  pl.BlockSpec((None, f, d), lambda e_, b_: (e_, 0, 0))],
        out_specs=[pl.BlockSpec((None, None, cap_l, d), lambda e_, b_: (b_, e_, 0, 0)),
                   pl.BlockSpec((None, None, cap_c, d), lambda e_, b_: (b_, e_, 0, 0))],
        out_shape=[jax.ShapeDtypeStruct((b, e, cap_l, d), BF16), jax.ShapeDtypeStruct((b, e, cap_c, d), BF16)],
        compiler_params=_cparams("parallel", "parallel"),
        name="expert_ffn",
    )(hb, slot4, aff4, wg, wu, wd)


def _combine_kernel(slot_ref, yl_ref, yc_ref, x_ref, mt_ref, o_ref, *, nct, cap_c, cap_l, n_exp):
    slot = slot_ref[...]
    tm = slot.shape[0]
    gate = mt_ref[5:6, :]

    @pl.when(pl.program_id(1) < nct)
    def _():
        lanes = lax.broadcasted_iota(I32, (tm, n_exp * cap_c), 1)
        hit = jnp.zeros((tm, n_exp * cap_c), F32)
        for e in range(n_exp):
            col = slot[:, e:e + 1]
            hit = hit + jnp.where((col >= 0) & (col + e * cap_c == lanes), 1.0, 0.0)
        moe = jnp.dot(hit.astype(BF16), yc_ref[...], preferred_element_type=F32)
        o_ref[...] = x_ref[...] + gate * moe

    @pl.when(pl.program_id(1) >= nct)
    def _():
        lanes = lax.broadcasted_iota(I32, (tm, cap_l), 1)
        parts = [jnp.where(slot[:, e:e + 1] == lanes, 1.0, 0.0).astype(BF16) for e in range(n_exp)]
        moe = jnp.dot(jnp.concatenate(parts, axis=1), yl_ref[...], preferred_element_type=F32)
        o_ref[...] = x_ref[...] + gate * moe


def _combine(slot_t, yl, yc, xs, mt, tm, nct, cap_c, cap_l):
    b, t, d = xs.shape
    e = slot_t.shape[2]
    x_spec, mt_spec = _row_specs(tm, d, nct)
    return pl.pallas_call(
        functools.partial(_combine_kernel, nct=nct, cap_c=cap_c, cap_l=cap_l, n_exp=e),
        grid=(b, t // tm),
        in_specs=[pl.BlockSpec((None, tm, e), lambda b, r: (b, r, 0)),
                  pl.BlockSpec((None, e * cap_l, d), lambda b, r: (b, 0, 0)),
                  pl.BlockSpec((None, e * cap_c, d), lambda b, r: (b, 0, 0)),
                  x_spec, mt_spec],
        out_specs=x_spec,
        out_shape=jax.ShapeDtypeStruct((b, t, d), F32),
        compiler_params=_cparams("parallel", "arbitrary"),
        name="moe_combine_residual",
    )(slot_t, yl, yc, xs, mt)


def _ec_moe(xs, mt, g, w_r, b_r, wg, wu, wd, tri, tm, nct, lc):
    b, t, d = xs.shape
    e = w_r.shape[1]
    cap_c = 2 * lc // e
    cap_l = 2 * (t - lc) // e
    hb, aff = _router(xs, mt, g, w_r.T, b_r.reshape(e, 1), tm, nct)
    slot = _topk(aff, tri, lc, cap_c, cap_l)
    yl, yc = _experts(hb, slot.reshape(b, e, 1, t), aff.reshape(b, e, 1, t), wg, wu, wd, lc, cap_c, cap_l)
    return _combine(jnp.swapaxes(slot, 1, 2), yl.reshape(b, e * cap_l, d), yc.reshape(b, e * cap_c, d),
                    xs, mt, tm, nct, cap_c, cap_l)


def _final_kernel(x_ref, g_ref, o_ref):
    x = x_ref[...]
    o_ref[...] = x * lax.rsqrt(jnp.mean(x * x, axis=-1, keepdims=True) + EPS) * g_ref[...]


def _final_norm(xs, g, tm, nct, l):
    b, t, d = xs.shape
    return pl.pallas_call(
        _final_kernel,
        grid=(b, l // tm),
        in_specs=[pl.BlockSpec((None, tm, d), lambda b, r: (b, r + nct, 0)), _full((1, d))],
        out_specs=pl.BlockSpec((None, tm, d), lambda b, r: (b, r, 0)),
        out_shape=jax.ShapeDtypeStruct((b, l, d), F32),
        compiler_params=_cparams("parallel", "parallel"),
        name="final_norm",
    )(xs, g)


def _rope_tables(lc, l):
    rows = l // GRID_W
    row = jnp.repeat(jnp.arange(rows, dtype=F32), GRID_W)
    col = jnp.tile(jnp.arange(GRID_W, dtype=F32), rows)
    inv = ROPE_BASE ** (-jnp.arange(ROPE_F, dtype=F32) / ROPE_F)
    ang = jnp.stack([row[:, None] * inv, col[:, None] * inv], axis=1)
    cos, sin = jnp.cos(ang), jnp.sin(ang)
    cos_l = jnp.concatenate([cos, cos], axis=-1).reshape(l, 4 * ROPE_F)
    sin_l = jnp.concatenate([-sin, sin], axis=-1).reshape(l, 4 * ROPE_F)
    cos_l = jnp.tile(cos_l, (1, LANES // (4 * ROPE_F)))
    sin_l = jnp.tile(sin_l, (1, LANES // (4 * ROPE_F)))
    cos_t = jnp.concatenate([jnp.ones((lc, LANES), F32), cos_l], axis=0)
    sin_t = jnp.concatenate([jnp.zeros((lc, LANES), F32), sin_l], axis=0)
    return cos_t, sin_t


def kernel(x, c, ctx, c_ctx, ada_w, ada_b, norm1_g, norm2_g, final_g, attn_w_qkv, attn_w_o, attn_lam_q1, attn_lam_k1, attn_lam_q2, attn_lam_k2, attn_subln_g, ssm_lam_re, ssm_lam_im, ssm_log_dt, ssm_b_re, ssm_b_im, ssm_c_re, ssm_c_im, ssm_d, ssm_w_glu1, ssm_w_glu2, moe_w_router, moe_b_router, moe_w_gate, moe_w_up, moe_w_down):
    b, l, d = x.shape
    lc = ctx.shape[1]
    depth = ada_w.shape[0]
    t = lc + l
    tm = 256 if lc % 256 == 0 and l % 256 == 0 else 128
    assert lc % tm == 0 and l % tm == 0 and lc % S5_CHUNK == 0 and l % S5_CHUNK == 0 and d == HEADS * V_DIM
    nct = lc // tm

    xs = jnp.concatenate([ctx, x], axis=1)
    rpad = -(b + 1) % 8
    cc = jnp.concatenate([c, c_ctx[None, :], jnp.zeros((rpad, d), F32)], axis=0)
    mods = _adaln(cc, ada_w, ada_b)
    cos_t, sin_t = _rope_tables(lc, l)
    tri = (jnp.arange(l)[:, None] < jnp.arange(l)[None, :]).astype(BF16)

    for i in range(depth):
        j = i // 2
        mod_l = mods[i, :b].reshape(b, 1, N_MOD, d)
        mod_c = jnp.broadcast_to(mods[i, b].reshape(1, 1, N_MOD, d), (b, 1, N_MOD, d))
        mt = jnp.pad(jnp.concatenate([mod_c, mod_l], axis=1), ((0, 0), (0, 0), (0, 8 - N_MOD), (0, 0)))
        g1 = norm1_g[i].reshape(1, d)
        g2 = norm2_g[i].reshape(1, d)
        if i % 2 == 0:
            lam_init = 0.8 - 0.6 * math.exp(-0.3 * i)
            qkv = _qkv_proj(xs, mt, g1, attn_w_qkv[j].astype(BF16), cos_t, sin_t, tm, nct)
            lam_vecs = jnp.zeros((8, LANES), F32).at[0:4, 0:HEAD_DIM].set(
                jnp.stack([attn_lam_q1[j], attn_lam_k1[j], attn_lam_q2[j], attn_lam_k2[j]]))
            o = _diff_attention(qkv, lam_vecs, attn_subln_g[j].reshape(1, V_DIM), lc, lam_init)
            xs = _proj_res(o, attn_w_o[j].astype(BF16), xs, mt, tm, nct)
        else:
            mats = _s5_matrices(ssm_lam_re[j], ssm_lam_im[j], ssm_log_dt[j], ssm_b_re[j], ssm_b_im[j],
                                ssm_c_re[j], ssm_c_im[j], ssm_d[j])
            h = _norm_mod(xs, mt, g1, tm, nct)
            a = _s5_mixer(h, mats, lc)
            xs = _glu_res(a, ssm_w_glu1[j].astype(BF16), ssm_w_glu2[j].astype(BF16), xs, mt, tm, nct)
        xs = _ec_moe(xs, mt, g2, moe_w_router[i], moe_b_router[i], moe_w_gate[i].astype(BF16),
                     moe_w_up[i].astype(BF16), moe_w_down[i].astype(BF16), tri, tm, nct, lc)
    return _final_norm(xs, final_g.reshape(1, d), tm, nct, l)
```

```python
import functools
import math

import jax
import jax.numpy as jnp
from jax import lax
from jax.experimental import pallas as pl
from jax.experimental.pallas import tpu as pltpu

F32 = jnp.float32
BF16 = jnp.bfloat16
I32 = jnp.int32

EPS = 1e-6
N_MOD = 6
HEADS = 8
HEAD_DIM = 64
V_DIM = 2 * HEAD_DIM
ROPE_BASE = 10000.0
ROPE_F = HEAD_DIM // 4
GRID_W = 64
LANES = 128
S5_CHUNK = LANES
VMEM_LIMIT = 56 * 1024 * 1024


def _cparams(*sem):
    return pltpu.CompilerParams(dimension_semantics=sem, vmem_limit_bytes=VMEM_LIMIT)


def _split(a):
    hi = a.astype(BF16)
    lo = (a - hi.astype(F32)).astype(BF16)
    return hi, lo


def _dot3(a, b, dims):
    ah, al = _split(a)
    bh, bl = _split(b)
    dn = (dims, ((), ()))
    d = functools.partial(lax.dot_general, dimension_numbers=dn, preferred_element_type=F32)
    return d(ah, bh) + (d(ah, bl) + d(al, bh))


def _rms_mod(x, g, shift, scale):
    ms = jnp.mean(x * x, axis=-1, keepdims=True)
    return (x * lax.rsqrt(ms + EPS) * g) * (1.0 + scale) + shift


def _mod_kernel(c_ref, w_ref, b_ref, o_ref):
    c = c_ref[...]
    s = c * jax.nn.sigmoid(c)
    o_ref[...] = _dot3(s, w_ref[...], ((1,), (0,))) + b_ref[...]


def _adaln(cc, ada_w, ada_b):
    depth, d, n = ada_w.shape
    r = cc.shape[0]
    tn = n // 4
    return pl.pallas_call(
        _mod_kernel,
        grid=(depth, n // tn),
        in_specs=[pl.BlockSpec((r, d), lambda i, j: (0, 0)),
                  pl.BlockSpec((None, d, tn), lambda i, j: (i, 0, j)),
                  pl.BlockSpec((None, 1, tn), lambda i, j: (i, 0, j))],
        out_specs=pl.BlockSpec((None, r, tn), lambda i, j: (i, 0, j)),
        out_shape=jax.ShapeDtypeStruct((depth, r, n), F32),
        compiler_params=_cparams("parallel", "parallel"),
        name="adaln_mod",
    )(cc, ada_w, ada_b.reshape(depth, 1, n))


def _row_specs(tm, d, nct):
    x_spec = pl.BlockSpec((None, tm, d), lambda b, r: (b, r, 0))
    mt_spec = pl.BlockSpec((None, None, 8, d), lambda b, r: (b, jnp.where(r < nct, 0, 1), 0, 0))
    return x_spec, mt_spec


def _full(shape):
    return pl.BlockSpec(shape, lambda b, r: (0,) * len(shape))


def _qkv_kernel(x_ref, mt_ref, g_ref, w_ref, cos_ref, sin_ref, o_ref, *, d, tn):
    h = _rms_mod(x_ref[...], g_ref[...], mt_ref[0:1, :], mt_ref[1:2, :]).astype(BF16)
    reps = tn // LANES
    cosw = jnp.concatenate([cos_ref[...]] * reps, axis=1)
    sinw = jnp.concatenate([sin_ref[...]] * reps, axis=1)
    lane = lax.broadcasted_iota(I32, (1, tn), 1)
    first_half = (lane % (2 * ROPE_F)) < ROPE_F
    for j in range(3 * d // tn):
        acc = jnp.dot(h, w_ref[:, j * tn:(j + 1) * tn], preferred_element_type=F32)
        if j * tn < 2 * d:
            partner = jnp.where(first_half, pltpu.roll(acc, tn - ROPE_F, 1), pltpu.roll(acc, ROPE_F, 1))
            acc = acc * cosw + partner * sinw
        if j * tn < d:
            acc = acc * (HEAD_DIM ** -0.5 * math.log2(math.e))
        o_ref[:, j * tn:(j + 1) * tn] = acc.astype(BF16)


def _qkv_proj(xs, mt, g, w, cos_t, sin_t, tm, nct):
    b, t, d = xs.shape
    tn = 512
    x_spec, mt_spec = _row_specs(tm, d, nct)
    return pl.pallas_call(
        functools.partial(_qkv_kernel, d=d, tn=tn),
        grid=(b, t // tm),
        in_specs=[x_spec, mt_spec, _full((1, d)), _full((d, 3 * d)),
                  pl.BlockSpec((tm, LANES), lambda b, r: (r, 0)),
                  pl.BlockSpec((tm, LANES), lambda b, r: (r, 0))],
        out_specs=pl.BlockSpec((None, tm, 3 * d), lambda b, r: (b, r, 0)),
        out_shape=jax.ShapeDtypeStruct((b, t, 3 * d), BF16),
        compiler_params=_cparams("parallel", "parallel"),
        name="norm_qkv_rope",
    )(xs, mt, g, w, cos_t, sin_t)


def _attn_kernel(q_ref, k_ref, v_ref, lam_ref, g_ref, o_ref, *, lc, tq, lam_init):
    lv = lam_ref[...]
    e1 = jnp.exp(jnp.sum(lv[0:1] * lv[1:2], axis=1, keepdims=True))
    e2 = jnp.exp(jnp.sum(lv[2:3] * lv[3:4], axis=1, keepdims=True))
    lam = e1 - e2 + lam_init
    g = g_ref[...] * (1.0 - lam_init)
    lane = lax.broadcasted_iota(I32, (1, V_DIM), 1)
    comp = (lane < HEAD_DIM, lane >= HEAD_DIM)
    nt = (((1,), (1,)), ((), ()))
    t = k_ref.shape[0]
    ones_col = jnp.where(lax.broadcasted_iota(I32, (t, V_DIM), 1) == 0, 1.0, 0.0).astype(BF16)
    v1 = jnp.concatenate([v_ref[...], ones_col], axis=1)
    kk = k_ref[...]

    def block(row0, nrows, nk):
        q = q_ref[pl.ds(row0, nrows), :]
        outs = []
        for c in range(2):
            qc = jnp.where(comp[c], q, jnp.zeros_like(q))
            s = lax.dot_general(qc, kk[0:nk], nt, preferred_element_type=F32)
            p = jnp.exp2(s - jnp.max(s, axis=-1, keepdims=True))
            acc = jnp.dot(p.astype(BF16), v1[0:nk], preferred_element_type=F32)
            outs.append(acc[:, 0:V_DIM] / acc[:, V_DIM:V_DIM + 1])
        o = outs[0] - lam * outs[1]
        o = o * lax.rsqrt(jnp.mean(o * o, axis=-1, keepdims=True) + EPS) * g
        o_ref[pl.ds(row0, nrows), :] = o.astype(BF16)

    tc = min(tq, lc)
    for r in range(lc // tc):
        block(r * tc, tc, lc)

    def body(i, carry):
        block(pl.multiple_of(lc + i * tq, tq), tq, t)
        return carry

    lax.fori_loop(0, (t - lc) // tq, body, 0, unroll=8)


def _diff_attention(qkv, lam_vecs, subln_g, lc, lam_init):
    b, t, d3 = qkv.shape
    d = d3 // 3
    tq = 256 if (t - lc) % 256 == 0 and lc % 256 == 0 else 128
    slab = lambda off: pl.BlockSpec((None, t, V_DIM), lambda b, h: (b, 0, off + h))
    return pl.pallas_call(
        functools.partial(_attn_kernel, lc=lc, tq=tq, lam_init=lam_init),
        grid=(b, HEADS),
        in_specs=[slab(0), slab(HEADS), slab(2 * HEADS),
                  pl.BlockSpec((8, LANES), lambda b, h: (0, 0)),
                  pl.BlockSpec((1, V_DIM), lambda b, h: (0, 0))],
        out_specs=pl.BlockSpec((None, t, V_DIM), lambda b, h: (b, 0, h)),
        out_shape=jax.ShapeDtypeStruct((b, t, d), BF16),
        compiler_params=_cparams("parallel", "parallel"),
        name="diff_attention",
    )(qkv, qkv, qkv, lam_vecs, subln_g)


def _proj_res_kernel(a_ref, w_ref, x_ref, mt_ref, o_ref):
    acc = jnp.dot(a_ref[...], w_ref[...], preferred_element_type=F32)
    o_ref[...] = x_ref[...] + mt_ref[2:3, :] * acc


def _proj_res(a, w, xs, mt, tm, nct):
    b, t, d = xs.shape
    x_spec, mt_spec = _row_specs(tm, d, nct)
    return pl.pallas_call(
        _proj_res_kernel,
        grid=(b, t // tm),
        in_specs=[x_spec, _full((d, d)), x_spec, mt_spec],
        out_specs=x_spec,
        out_shape=jax.ShapeDtypeStruct((b, t, d), F32),
        compiler_params=_cparams("parallel", "parallel"),
        name="out_proj_residual",
    )(a, w, xs, mt)


def _s5_pack_kernel(x_ref, mt_ref, g_ref, o_ref, *, ncc):
    is_ctx = pl.program_id(0) < ncc
    for bi in range(x_ref.shape[0]):
        m = jnp.where(is_ctx, mt_ref[bi, 0], mt_ref[bi, 1])
        h = _rms_mod(x_ref[bi], g_ref[...], m[0:1, :], m[1:2, :])
        o_ref[:, bi, :] = h.T


def _s5_pack(xs, mt, g, ncc, bb):
    b, t, d = xs.shape
    tc = S5_CHUNK
    nc, nbb = t // tc, b // bb
    return pl.pallas_call(
        functools.partial(_s5_pack_kernel, ncc=ncc),
        grid=(nc, nbb),
        in_specs=[pl.BlockSpec((bb, tc, d), lambda c, h: (h, c, 0)),
                  pl.BlockSpec((bb, 2, 8, d), lambda c, h: (h, 0, 0, 0)),
                  pl.BlockSpec((1, d), lambda c, h: (0, 0))],
        out_specs=pl.BlockSpec((d, bb, tc), lambda c, h: (0, c * nbb + h, 0)),
        out_shape=jax.ShapeDtypeStruct((d, nc * b, tc), F32),
        compiler_params=_cparams("parallel", "parallel"),
        name="s5_norm_pack",
    )(xs, mt, g)


def _s5_group_kernel(u_ref, cbf_ref, cbb_ref, pwf_ref, pwb_ref, pa_ref, pb_ref, ba_ref, bb_ref,
                     qa_ref, qb_ref, ca_ref, cb_ref, dec_ref, dsk_ref, o_ref, m_scr, in_scr, st_scr, kf_scr, kb_scr,
                     *, nb, nc, ncc):
    tc = S5_CHUNK
    p = u_ref.shape[0]
    u = jnp.concatenate([u_ref[i] for i in range(p)], axis=1)
    ub = u.astype(BF16)

    m_in = jnp.concatenate(
        [(pa_ref[...] * ba_ref[i:i + 1, :] + pb_ref[...] * bb_ref[i:i + 1, :]).astype(BF16) for i in range(p)], axis=0)
    in_scr[...] = jnp.dot(ub, m_in, preferred_element_type=F32)

    a1 = dec_ref[0:1, :]
    a2 = dec_ref[1:2, :]

    def step(s, inc, lo):
        sl = slice(lo, lo + LANES)
        return a1[:, sl] * s + a2[:, sl] * pltpu.roll(s, LANES // 2, 1) + inc

    def scan_body(i, carry):
        s0, s1 = carry
        r0 = pl.multiple_of(i * nb, nb)
        c1 = jnp.where(i < ncc, ncc - 1 - i, nc + ncc - 1 - i)
        r1 = pl.multiple_of(c1 * nb, nb)
        st_scr[pl.ds(r0, nb), 0:LANES] = s0
        st_scr[pl.ds(r1, nb), LANES:2 * LANES] = s1
        s0 = step(s0, in_scr[pl.ds(r0, nb), 0:LANES], 0)
        s1 = step(s1, in_scr[pl.ds(r1, nb), LANES:2 * LANES], LANES)
        return s0, s1

    z = jnp.zeros((nb, LANES), F32)
    lax.fori_loop(0, nc, scan_body, (z, z))

    kf = _dot3(cbf_ref[...], pwf_ref[...], ((1,), (0,)))
    kb = _dot3(cbb_ref[...], pwb_ref[...], ((1,), (0,)))
    kf = kf + jnp.where(lax.broadcasted_iota(I32, kf.shape, 1) == 0, kb, 0.0)
    kf_scr[...] = kf
    kb_scr[...] = kb
    causal = lax.broadcasted_iota(I32, (tc, tc), 1) >= lax.broadcasted_iota(I32, (tc, tc), 0)

    def toeplitz_body(i, carry):
        for o in range(p):
            r = i * p + o
            f = jnp.broadcast_to(kf_scr[pl.ds(r, 1), :], (tc, tc))
            b = jnp.broadcast_to(kb_scr[pl.ds(r, 1), :], (tc, tc))
            f = pltpu.roll(f, 0, 1, stride=1, stride_axis=0)
            b = pltpu.roll(b, 0, 1, stride=1, stride_axis=0)
            m_scr[pl.ds(pl.multiple_of(i * tc, tc), tc), o * tc:(o + 1) * tc] = jnp.where(causal, f, b).astype(BF16)
        return carry

    lax.fori_loop(0, p, toeplitz_body, 0)

    m_out = jnp.concatenate(
        [(qa_ref[...] * ca_ref[:, o:o + 1] + qb_ref[...] * cb_ref[:, o:o + 1]).astype(BF16) for o in range(p)], axis=1)
    y = jnp.dot(ub, m_scr[...], preferred_element_type=F32)
    y = y + jnp.dot(st_scr[...].astype(BF16), m_out, preferred_element_type=F32)
    y = jax.nn.gelu(y + dsk_ref[...] * u)
    for o in range(p):
        o_ref[o] = y[:, o * tc:(o + 1) * tc]


def _s5_tables(lam_re, lam_im, log_dt, b_re, b_im, c_re, c_im, d_skip):
    tc = S5_CHUNK
    ndir, g, n = lam_re.shape
    p = b_re.shape[-1]
    dt = jnp.exp(log_dt)[..., None]
    mag = jnp.exp(lam_re * dt)
    ang = lam_im * dt
    ab_re, ab_im = mag * jnp.cos(ang), mag * jnp.sin(ang)
    den = lam_re * lam_re + lam_im * lam_im
    nr, ni = ab_re - 1.0, ab_im
    coef_re = (nr * lam_re + ni * lam_im) / den
    coef_im = (ni * lam_re - nr * lam_im) / den
    bb_re = coef_re[..., None] * b_re - coef_im[..., None] * b_im
    bb_im = coef_re[..., None] * b_im + coef_im[..., None] * b_re
    k = jnp.arange(tc + 1, dtype=F32)[None, None, :, None]
    pw_mag = jnp.exp(k * (lam_re * dt)[:, :, None, :])
    pw_re = pw_mag * jnp.cos(k * ang[:, :, None, :])
    pw_im = pw_mag * jnp.sin(k * ang[:, :, None, :])
    cb_re = (c_re[:, :, None, :, :] * bb_re.transpose(0, 1, 3, 2)[:, :, :, None, :]
             - c_im[:, :, None, :, :] * bb_im.transpose(0, 1, 3, 2)[:, :, :, None, :])
    cb_im = (c_re[:, :, None, :, :] * bb_im.transpose(0, 1, 3, 2)[:, :, :, None, :]
             + c_im[:, :, None, :, :] * bb_re.transpose(0, 1, 3, 2)[:, :, :, None, :])
    cb = jnp.concatenate([cb_re, -cb_im], axis=-1).reshape(ndir, g, p * p, 2 * n)
    lag_f = jnp.arange(tc)
    lag_b = jnp.concatenate([jnp.zeros((1,), jnp.int32), tc - jnp.arange(1, tc)])
    rows = lambda a: a.transpose(0, 2, 1)
    pwf = jnp.concatenate([rows(pw_re[0][:, lag_f]), rows(pw_im[0][:, lag_f])], axis=1)
    pwb = jnp.concatenate([rows(pw_re[1][:, lag_b]), rows(pw_im[1][:, lag_b])], axis=1)
    f_re, f_im = pw_re[0][:, tc - 1::-1][:, :tc], pw_im[0][:, tc - 1::-1][:, :tc]
    r_re, r_im = pw_re[1][:, :tc], pw_im[1][:, :tc]
    pa = jnp.concatenate([f_re, f_re, r_re, r_re], axis=-1)
    pb = jnp.concatenate([-f_im, f_im, -r_im, r_im], axis=-1)
    bt_re, bt_im = bb_re.transpose(0, 1, 3, 2), bb_im.transpose(0, 1, 3, 2)
    ba = jnp.concatenate([bt_re[0], bt_im[0], bt_re[1], bt_im[1]], axis=-1)
    bb = jnp.concatenate([bt_im[0], bt_re[0], bt_im[1], bt_re[1]], axis=-1)
    o_re0, o_im0 = rows(pw_re[0][:, 1:]), rows(pw_im[0][:, 1:])
    o_re1, o_im1 = rows(pw_re[1][:, tc:0:-1]), rows(pw_im[1][:, tc:0:-1])
    qa = jnp.concatenate([o_re0, -o_im0, o_re1, -o_im1], axis=1)
    qb = jnp.concatenate([-o_im0, -o_re0, -o_im1, -o_re1], axis=1)
    ct_re, ct_im = c_re.transpose(0, 1, 3, 2), c_im.transpose(0, 1, 3, 2)
    ca = jnp.concatenate([ct_re[0], ct_re[0], ct_re[1], ct_re[1]], axis=1)
    cbm = jnp.concatenate([ct_im[0], ct_im[0], ct_im[1], ct_im[1]], axis=1)
    a_re, a_im = pw_re[:, :, tc], pw_im[:, :, tc]
    a1 = jnp.concatenate([a_re[0], a_re[0], a_re[1], a_re[1]], axis=-1)
    a2 = jnp.concatenate([-a_im[0], a_im[0], -a_im[1], a_im[1]], axis=-1)
    decay = jnp.stack([a1, a2], axis=1)
    dsk = jnp.repeat(d_skip.reshape(g, 1, p), tc, axis=2)
    return cb[0], cb[1], pwf, pwb, pa, pb, ba, bb, qa, qb, ca, cbm, decay, dsk


def _s5_groups(act, tables, nb, ncc):
    d, rows, tc = act.shape
    g = tables[0].shape[0]
    p = d // g
    nc = rows // nb
    per_g = lambda a: pl.BlockSpec((None,) + a.shape[1:], lambda i: (i,) + (0,) * (a.ndim - 1))
    slab = pl.BlockSpec((p, rows, tc), lambda i: (i, 0, 0))
    sw = tables[4].shape[2]
    return pl.pallas_call(
        functools.partial(_s5_group_kernel, nb=nb, nc=nc, ncc=ncc),
        grid=(g,),
        in_specs=[slab] + [per_g(a) for a in tables],
        out_specs=slab,
        out_shape=jax.ShapeDtypeStruct((d, rows, tc), F32),
        scratch_shapes=[pltpu.VMEM((p * tc, p * tc), BF16), pltpu.VMEM((rows, sw), F32), pltpu.VMEM((rows, sw), F32),
                        pltpu.VMEM((p * p, tc), F32), pltpu.VMEM((p * p, tc), F32)],
        compiler_params=_cparams("parallel"),
        name="s5_group_mix",
    )(act, *tables)


def _s5_glu_kernel(y_ref, w1_ref, w2_ref, x_ref, mt_ref, o_ref, *, ncc):
    is_ctx = pl.program_id(0) < ncc
    nbb = y_ref.shape[1]
    a = jnp.concatenate([y_ref[:, bi, :].T.astype(BF16) for bi in range(nbb)], axis=0)
    z1 = jnp.dot(a, w1_ref[...], preferred_element_type=F32)
    z2 = jnp.dot(a, w2_ref[...], preferred_element_type=F32)
    z = z1 * jax.nn.sigmoid(z2)
    tc = y_ref.shape[2]
    for bi in range(nbb):
        gate = jnp.where(is_ctx, mt_ref[bi, 0], mt_ref[bi, 1])[2:3, :]
        o_ref[bi] = x_ref[bi] + gate * z[bi * tc:(bi + 1) * tc]


def _s5_glu(y, w1, w2, xs, mt, ncc, bb):
    b, t, d = xs.shape
    tc = S5_CHUNK
    nc, nbb = t // tc, b // bb
    x_spec = pl.BlockSpec((bb, tc, d), lambda c, h: (h, c, 0))
    return pl.pallas_call(
        functools.partial(_s5_glu_kernel, ncc=ncc),
        grid=(nc, nbb),
        in_specs=[pl.BlockSpec((d, bb, tc), lambda c, h: (0, c * nbb + h, 0)),
                  pl.BlockSpec((d, d), lambda c, h: (0, 0)), pl.BlockSpec((d, d), lambda c, h: (0, 0)),
                  x_spec, pl.BlockSpec((bb, 2, 8, d), lambda c, h: (h, 0, 0, 0))],
        out_specs=x_spec,
        out_shape=jax.ShapeDtypeStruct((b, t, d), F32),
        compiler_params=_cparams("parallel", "parallel"),
        name="s5_unpack_glu_residual",
    )(y, w1, w2, xs, mt)


def _router_kernel(x_ref, mt_ref, g_ref, wr_ref, br_ref, h_ref, aff_ref):
    h = _rms_mod(x_ref[...], g_ref[...], mt_ref[3:4, :], mt_ref[4:5, :])
    h_ref[...] = h.astype(BF16)
    logits = _dot3(wr_ref[...], h, ((1,), (1,))) + br_ref[...]
    p = jnp.exp(logits - jnp.max(logits, axis=0, keepdims=True))
    aff_ref[...] = p / jnp.sum(p, axis=0, keepdims=True)


def _router(xs, mt, g, w_rt, b_r, tm, nct):
    b, t, d = xs.shape
    e = w_rt.shape[0]
    x_spec, mt_spec = _row_specs(tm, d, nct)
    return pl.pallas_call(
        _router_kernel,
        grid=(b, t // tm),
        in_specs=[x_spec, mt_spec, _full((1, d)), _full((e, d)), _full((e, 1))],
        out_specs=[x_spec, pl.BlockSpec((None, e, tm), lambda b, r: (b, 0, r))],
        out_shape=[jax.ShapeDtypeStruct((b, t, d), BF16), jax.ShapeDtypeStruct((b, e, t), F32)],
        compiler_params=_cparams("parallel", "parallel"),
        name="norm_router",
    )(xs, mt, g, w_rt, b_r)


def _select_slots(aff, cap, tri):
    e, n = aff.shape
    bits = pltpu.bitcast(aff, I32)
    capf = float(cap)

    def body(_, carry):
        lo, hi = carry
        mid = lo + ((hi - lo) >> 1)
        cnt = jnp.sum(jnp.where(bits >= mid, 1.0, 0.0), axis=1, keepdims=True)
        ge = cnt >= capf
        return jnp.where(ge, mid, lo), jnp.where(ge, hi, mid)

    lo0 = jnp.zeros((e, 1), I32)
    hi0 = jnp.full((e, 1), 0x7F800001, I32)
    thr, _ = lax.fori_loop(0, 31, body, (lo0, hi0))
    gt = bits > thr
    eq = bits == thr
    need = capf - jnp.sum(jnp.where(gt, 1.0, 0.0), axis=1, keepdims=True)
    eq_rank = jnp.dot(jnp.where(eq, 1.0, 0.0).astype(BF16), tri, preferred_element_type=F32)
    sel = jnp.where(gt, 1.0, jnp.where(eq, jnp.where(eq_rank < need, 1.0, 0.0), 0.0))
    rank = jnp.dot(sel.astype(BF16), tri, preferred_element_type=F32)
    return jnp.where(sel > 0.5, rank.astype(I32), -1)


def _topk_kernel(aff_ref, tri_ref, slot_ref, *, lc, cap_c, cap_l):
    slot_ref[:, 0:lc] = _select_slots(aff_ref[:, 0:lc], cap_c, tri_ref[0:lc, 0:lc])
    slot_ref[:, lc:] = _select_slots(aff_ref[:, lc:], cap_l, tri_ref[...])


def _topk(aff, tri, lc, cap_c, cap_l):
    b, e, t = aff.shape
    l = t - lc
    return pl.pallas_call(
        functools.partial(_topk_kernel, lc=lc, cap_c=cap_c, cap_l=cap_l),
        grid=(b,),
        in_specs=[pl.BlockSpec((None, e, t), lambda i: (i, 0, 0)), pl.BlockSpec((l, l), lambda i: (0, 0))],
        out_specs=pl.BlockSpec((None, e, t), lambda i: (i, 0, 0)),
        out_shape=jax.ShapeDtypeStruct((b, e, t), I32),
        compiler_params=_cparams("parallel"),
        name="expert_choice_topk",
    )(aff, tri)


def _expert_kernel(h_ref, slot_ref, aff_ref, wg_ref, wu_ref, wd_ref, yl_ref, yc_ref, *, lc, cap_c, cap_l):
    slot = slot_ref[...]
    aff = aff_ref[...]

    def gather(lo, hi, cap):
        n = hi - lo
        hit = lax.broadcasted_iota(I32, (cap, n), 0) == slot[:, lo:hi]
        onehot = jnp.where(hit, 1.0, 0.0).astype(BF16)
        xin = jnp.dot(onehot, h_ref[lo:hi, :], preferred_element_type=F32).astype(BF16)
        w = jnp.sum(jnp.where(hit, aff[:, lo:hi], 0.0), axis=1, keepdims=True)
        return xin, w

    xl, wl = gather(lc, h_ref.shape[0], cap_l)
    xc, wc = gather(0, lc, cap_c)
    xin = jnp.concatenate([xl, xc], axis=0)
    gate = jnp.dot(xin, wg_ref[...], preferred_element_type=F32)
    up = jnp.dot(xin, wu_ref[...], preferred_element_type=F32)
    hid = (gate * jax.nn.sigmoid(gate) * up).astype(BF16)
    y = jnp.dot(hid, wd_ref[...], preferred_element_type=F32)
    yl_ref[...] = (y[0:cap_l] * wl).astype(BF16)
    yc_ref[...] = (y[cap_l:] * wc).astype(BF16)


def _experts(hb, slot4, aff4, wg, wu, wd, lc, cap_c, cap_l):
    b, t, d = hb.shape
    e, _, f = wg.shape
    row = pl.BlockSpec((None, None, 1, t), lambda e_, b_: (b_, e_, 0, 0))
    return pl.pallas_call(
        functools.partial(_expert_kernel, lc=lc, cap_c=cap_c, cap_l=cap_l),
        grid=(e, b),
        in_specs=[pl.BlockSpec((None, t, d), lambda e_, b_: (b_, 0, 0)), row, row,
                  pl.BlockSpec((None, d, f), lambda e_, b_: (e_, 0, 0)),
                  pl.BlockSpec((None, d, f), lambda e_, b_: (e_, 0, 0)),
                  pl.BlockSpec((None, f, d), lambda e_, b_: (e_, 0, 0))],
        out_specs=[pl.BlockSpec((None, None, cap_l, d), lambda e_, b_: (b_, e_, 0, 0)),
                   pl.BlockSpec((None, None, cap_c, d), lambda e_, b_: (b_, e_, 0, 0))],
        out_shape=[jax.ShapeDtypeStruct((b, e, cap_l, d), BF16), jax.ShapeDtypeStruct((b, e, cap_c, d), BF16)],
        compiler_params=_cparams("parallel", "parallel"),
        name="expert_ffn",
    )(hb, slot4, aff4, wg, wu, wd)


def _combine_kernel(slot_ref, yl_ref, yc_ref, x_ref, mt_ref, o_ref, *, nct, cap_c, cap_l, n_exp):
    slot = slot_ref[...]
    tm = slot.shape[0]
    gate = mt_ref[5:6, :]

    @pl.when(pl.program_id(1) < nct)
    def _():
        lanes = lax.broadcasted_iota(I32, (tm, n_exp * cap_c), 1)
        hit = jnp.zeros((tm, n_exp * cap_c), F32)
        for e in range(n_exp):
            col = slot[:, e:e + 1]
            hit = hit + jnp.where((col >= 0) & (col + e * cap_c == lanes), 1.0, 0.0)
        moe = jnp.dot(hit.astype(BF16), yc_ref[...], preferred_element_type=F32)
        o_ref[...] = x_ref[...] + gate * moe

    @pl.when(pl.program_id(1) >= nct)
    def _():
        lanes = lax.broadcasted_iota(I32, (tm, cap_l), 1)
        parts = [jnp.where(slot[:, e:e + 1] == lanes, 1.0, 0.0).astype(BF16) for e in range(n_exp)]
        moe = jnp.dot(jnp.concatenate(parts, axis=1), yl_ref[...], preferred_element_type=F32)
        o_ref[...] = x_ref[...] + gate * moe


def _combine(slot_t, yl, yc, xs, mt, tm, nct, cap_c, cap_l):
    b, t, d = xs.shape
    e = slot_t.shape[2]
    x_spec, mt_spec = _row_specs(tm, d, nct)
    return pl.pallas_call(
        functools.partial(_combine_kernel, nct=nct, cap_c=cap_c, cap_l=cap_l, n_exp=e),
        grid=(b, t // tm),
        in_specs=[pl.BlockSpec((None, tm, e), lambda b, r: (b, r, 0)),
                  pl.BlockSpec((None, e * cap_l, d), lambda b, r: (b, 0, 0)),
                  pl.BlockSpec((None, e * cap_c, d), lambda b, r: (b, 0, 0)),
                  x_spec, mt_spec],
        out_specs=x_spec,
        out_shape=jax.ShapeDtypeStruct((b, t, d), F32),
        compiler_params=_cparams("parallel", "arbitrary"),
        name="moe_combine_residual",
    )(slot_t, yl, yc, xs, mt)


def _ec_moe(xs, mt, g, w_r, b_r, wg, wu, wd, tri, tm, nct, lc):
    b, t, d = xs.shape
    e = w_r.shape[1]
    cap_c = 2 * lc // e
    cap_l = 2 * (t - lc) // e
    hb, aff = _router(xs, mt, g, w_r.T, b_r.reshape(e, 1), tm, nct)
    slot = _topk(aff, tri, lc, cap_c, cap_l)
    yl, yc = _experts(hb, slot.reshape(b, e, 1, t), aff.reshape(b, e, 1, t), wg, wu, wd, lc, cap_c, cap_l)
    return _combine(jnp.swapaxes(slot, 1, 2), yl.reshape(b, e * cap_l, d), yc.reshape(b, e * cap_c, d),
                    xs, mt, tm, nct, cap_c, cap_l)


def _final_kernel(x_ref, g_ref, o_ref):
    x = x_ref[...]
    o_ref[...] = x * lax.rsqrt(jnp.mean(x * x, axis=-1, keepdims=True) + EPS) * g_ref[...]


def _final_norm(xs, g, tm, nct, l):
    b, t, d = xs.shape
    return pl.pallas_call(
        _final_kernel,
        grid=(b, l // tm),
        in_specs=[pl.BlockSpec((None, tm, d), lambda b, r: (b, r + nct, 0)), _full((1, d))],
        out_specs=pl.BlockSpec((None, tm, d), lambda b, r: (b, r, 0)),
        out_shape=jax.ShapeDtypeStruct((b, l, d), F32),
        compiler_params=_cparams("parallel", "parallel"),
        name="final_norm",
    )(xs, g)


def _rope_tables(lc, l):
    rows = l // GRID_W
    row = jnp.repeat(jnp.arange(rows, dtype=F32), GRID_W)
    col = jnp.tile(jnp.arange(GRID_W, dtype=F32), rows)
    inv = ROPE_BASE ** (-jnp.arange(ROPE_F, dtype=F32) / ROPE_F)
    ang = jnp.stack([row[:, None] * inv, col[:, None] * inv], axis=1)
    cos, sin = jnp.cos(ang), jnp.sin(ang)
    cos_l = jnp.concatenate([cos, cos], axis=-1).reshape(l, 4 * ROPE_F)
    sin_l = jnp.concatenate([-sin, sin], axis=-1).reshape(l, 4 * ROPE_F)
    cos_l = jnp.tile(cos_l, (1, LANES // (4 * ROPE_F)))
    sin_l = jnp.tile(sin_l, (1, LANES // (4 * ROPE_F)))
    cos_t = jnp.concatenate([jnp.ones((lc, LANES), F32), cos_l], axis=0)
    sin_t = jnp.concatenate([jnp.zeros((lc, LANES), F32), sin_l], axis=0)
    return cos_t, sin_t


def kernel(x, c, ctx, c_ctx, ada_w, ada_b, norm1_g, norm2_g, final_g, attn_w_qkv, attn_w_o, attn_lam_q1, attn_lam_k1, attn_lam_q2, attn_lam_k2, attn_subln_g, ssm_lam_re, ssm_lam_im, ssm_log_dt, ssm_b_re, ssm_b_im, ssm_c_re, ssm_c_im, ssm_d, ssm_w_glu1, ssm_w_glu2, moe_w_router, moe_b_router, moe_w_gate, moe_w_up, moe_w_down):
    b, l, d = x.shape
    lc = ctx.shape[1]
    depth = ada_w.shape[0]
    t = lc + l
    tm = 256 if lc % 256 == 0 and l % 256 == 0 else 128
    assert lc % tm == 0 and l % tm == 0 and lc % S5_CHUNK == 0 and l % S5_CHUNK == 0 and d == HEADS * V_DIM
    nct = lc // tm

    xs = jnp.concatenate([ctx, x], axis=1)
    rpad = -(b + 1) % 8
    cc = jnp.concatenate([c, c_ctx[None, :], jnp.zeros((rpad, d), F32)], axis=0)
    mods = _adaln(cc, ada_w, ada_b)
    cos_t, sin_t = _rope_tables(lc, l)
    tri = (jnp.arange(l)[:, None] < jnp.arange(l)[None, :]).astype(BF16)

    for i in range(depth):
        j = i // 2
        mod_l = mods[i, :b].reshape(b, 1, N_MOD, d)
        mod_c = jnp.broadcast_to(mods[i, b].reshape(1, 1, N_MOD, d), (b, 1, N_MOD, d))
        mt = jnp.pad(jnp.concatenate([mod_c, mod_l], axis=1), ((0, 0), (0, 0), (0, 8 - N_MOD), (0, 0)))
        g1 = norm1_g[i].reshape(1, d)
        g2 = norm2_g[i].reshape(1, d)
        if i % 2 == 0:
            lam_init = 0.8 - 0.6 * math.exp(-0.3 * i)
            qkv = _qkv_proj(xs, mt, g1, attn_w_qkv[j].astype(BF16), cos_t, sin_t, tm, nct)
            lam_vecs = jnp.zeros((8, LANES), F32).at[0:4, 0:HEAD_DIM].set(
                jnp.stack([attn_lam_q1[j], attn_lam_k1[j], attn_lam_q2[j], attn_lam_k2[j]]))
            o = _diff_attention(qkv, lam_vecs, attn_subln_g[j].reshape(1, V_DIM), lc, lam_init)
            xs = _proj_res(o, attn_w_o[j].astype(BF16), xs, mt, tm, nct)
        else:
            tables = _s5_tables(ssm_lam_re[j], ssm_lam_im[j], ssm_log_dt[j], ssm_b_re[j], ssm_b_im[j],
                                ssm_c_re[j], ssm_c_im[j], ssm_d[j])
            ncc = lc // S5_CHUNK
            bb = 8 if b % 8 == 0 else b
            act = _s5_pack(xs, mt, g1, ncc, bb)
            y = _s5_groups(act, tables, b, ncc)
            xs = _s5_glu(y, ssm_w_glu1[j].astype(BF16), ssm_w_glu2[j].astype(BF16), xs, mt, ncc, bb)
        xs = _ec_moe(xs, mt, g2, moe_w_router[i], moe_b_router[i], moe_w_gate[i].astype(BF16),
                     moe_w_up[i].astype(BF16), moe_w_down[i].astype(BF16), tri, tm, nct, lc)
    return _final_norm(xs, final_g.reshape(1, d), tm, nct, l)
```

```python
import functools
import math

import jax
import jax.numpy as jnp
from jax import lax
from jax.experimental import pallas as pl
from jax.experimental.pallas import tpu as pltpu

F32 = jnp.float32
BF16 = jnp.bfloat16
I32 = jnp.int32

EPS = 1e-6
N_MOD = 6
HEADS = 8
HEAD_DIM = 64
V_DIM = 2 * HEAD_DIM
ROPE_BASE = 10000.0
ROPE_F = HEAD_DIM // 4
GRID_W = 64
LANES = 128
S5_CHUNK = LANES
VMEM_LIMIT = 56 * 1024 * 1024


def _cparams(*sem):
    return pltpu.CompilerParams(dimension_semantics=sem, vmem_limit_bytes=VMEM_LIMIT)


def _split(a):
    hi = a.astype(BF16)
    lo = (a - hi.astype(F32)).astype(BF16)
    return hi, lo


def _dot3(a, b, dims):
    ah, al = _split(a)
    bh, bl = _split(b)
    dn = (dims, ((), ()))
    d = functools.partial(lax.dot_general, dimension_numbers=dn, preferred_element_type=F32)
    return d(ah, bh) + (d(ah, bl) + d(al, bh))


def _rms_mod(x, g, shift, scale):
    ms = jnp.mean(x * x, axis=-1, keepdims=True)
    return (x * lax.rsqrt(ms + EPS) * g) * (1.0 + scale) + shift


def _mod_kernel(c_ref, w_ref, b_ref, o_ref):
    c = c_ref[...]
    s = c * jax.nn.sigmoid(c)
    o_ref[...] = _dot3(s, w_ref[...], ((1,), (0,))) + b_ref[...]


def _adaln(cc, ada_w, ada_b):
    depth, d, n = ada_w.shape
    r = cc.shape[0]
    tn = n // 4
    return pl.pallas_call(
        _mod_kernel,
        grid=(depth, n // tn),
        in_specs=[pl.BlockSpec((r, d), lambda i, j: (0, 0)),
                  pl.BlockSpec((None, d, tn), lambda i, j: (i, 0, j)),
                  pl.BlockSpec((None, 1, tn), lambda i, j: (i, 0, j))],
        out_specs=pl.BlockSpec((None, r, tn), lambda i, j: (i, 0, j)),
        out_shape=jax.ShapeDtypeStruct((depth, r, n), F32),
        compiler_params=_cparams("parallel", "parallel"),
        name="adaln_mod",
    )(cc, ada_w, ada_b.reshape(depth, 1, n))


def _row_specs(tm, d, nct):
    x_spec = pl.BlockSpec((None, tm, d), lambda b, r: (b, r, 0))
    mt_spec = pl.BlockSpec((None, None, 8, d), lambda b, r: (b, jnp.where(r < nct, 0, 1), 0, 0))
    return x_spec, mt_spec


def _full(shape):
    return pl.BlockSpec(shape, lambda b, r: (0,) * len(shape))


def _qkv_kernel(x_ref, mt_ref, g_ref, w_ref, cos_ref, sin_ref, o_ref, *, d, tn):
    h = _rms_mod(x_ref[...], g_ref[...], mt_ref[0:1, :], mt_ref[1:2, :]).astype(BF16)
    reps = tn // LANES
    cosw = jnp.concatenate([cos_ref[...]] * reps, axis=1)
    sinw = jnp.concatenate([sin_ref[...]] * reps, axis=1)
    lane = lax.broadcasted_iota(I32, (1, tn), 1)
    first_half = (lane % (2 * ROPE_F)) < ROPE_F
    for j in range(3 * d // tn):
        acc = jnp.dot(h, w_ref[:, j * tn:(j + 1) * tn], preferred_element_type=F32)
        if j * tn < 2 * d:
            partner = jnp.where(first_half, pltpu.roll(acc, tn - ROPE_F, 1), pltpu.roll(acc, ROPE_F, 1))
            acc = acc * cosw + partner * sinw
        if j * tn < d:
            acc = acc * (HEAD_DIM ** -0.5 * math.log2(math.e))
        o_ref[:, j * tn:(j + 1) * tn] = acc.astype(BF16)


def _qkv_proj(xs, mt, g, w, cos_t, sin_t, tm, nct):
    b, t, d = xs.shape
    tn = 512
    x_spec, mt_spec = _row_specs(tm, d, nct)
    return pl.pallas_call(
        functools.partial(_qkv_kernel, d=d, tn=tn),
        grid=(b, t // tm),
        in_specs=[x_spec, mt_spec, _full((1, d)), _full((d, 3 * d)),
                  pl.BlockSpec((tm, LANES), lambda b, r: (r, 0)),
                  pl.BlockSpec((tm, LANES), lambda b, r: (r, 0))],
        out_specs=pl.BlockSpec((None, tm, 3 * d), lambda b, r: (b, r, 0)),
        out_shape=jax.ShapeDtypeStruct((b, t, 3 * d), BF16),
        compiler_params=_cparams("parallel", "parallel"),
        name="norm_qkv_rope",
    )(xs, mt, g, w, cos_t, sin_t)


def _attn_kernel(q_ref, k_ref, v_ref, lam_ref, g_ref, o_ref, *, lc, tq, lam_init):
    lv = lam_ref[...]
    e1 = jnp.exp(jnp.sum(lv[0:1] * lv[1:2], axis=1, keepdims=True))
    e2 = jnp.exp(jnp.sum(lv[2:3] * lv[3:4], axis=1, keepdims=True))
    lam = e1 - e2 + lam_init
    g = g_ref[...] * (1.0 - lam_init)
    lane = lax.broadcasted_iota(I32, (1, V_DIM), 1)
    comp = (lane < HEAD_DIM, lane >= HEAD_DIM)
    nt = (((1,), (1,)), ((), ()))
    t = k_ref.shape[0]
    ones_col = jnp.where(lax.broadcasted_iota(I32, (t, V_DIM), 1) == 0, 1.0, 0.0).astype(BF16)
    v1 = jnp.concatenate([v_ref[...], ones_col], axis=1)
    kk = k_ref[...]

    def block(row0, nrows, nk):
        q = q_ref[pl.ds(row0, nrows), :]
        outs = []
        for c in range(2):
            qc = jnp.where(comp[c], q, jnp.zeros_like(q))
            s = lax.dot_general(qc, kk[0:nk], nt, preferred_element_type=F32)
            p = jnp.exp2(s - jnp.max(s, axis=-1, keepdims=True))
            acc = jnp.dot(p.astype(BF16), v1[0:nk], preferred_element_type=F32)
            outs.append(acc[:, 0:V_DIM] / acc[:, V_DIM:V_DIM + 1])
        o = outs[0] - lam * outs[1]
        o = o * lax.rsqrt(jnp.mean(o * o, axis=-1, keepdims=True) + EPS) * g
        o_ref[pl.ds(row0, nrows), :] = o.astype(BF16)

    tc = min(tq, lc)
    for r in range(lc // tc):
        block(r * tc, tc, lc)

    def body(i, carry):
        block(pl.multiple_of(lc + i * tq, tq), tq, t)
        return carry

    lax.fori_loop(0, (t - lc) // tq, body, 0, unroll=8)


def _diff_attention(qkv, lam_vecs, subln_g, lc, lam_init):
    b, t, d3 = qkv.shape
    d = d3 // 3
    tq = 256 if (t - lc) % 256 == 0 and lc % 256 == 0 else 128
    slab = lambda off: pl.BlockSpec((None, t, V_DIM), lambda b, h: (b, 0, off + h))
    return pl.pallas_call(
        functools.partial(_attn_kernel, lc=lc, tq=tq, lam_init=lam_init),
        grid=(b, HEADS),
        in_specs=[slab(0), slab(HEADS), slab(2 * HEADS),
                  pl.BlockSpec((8, LANES), lambda b, h: (0, 0)),
                  pl.BlockSpec((1, V_DIM), lambda b, h: (0, 0))],
        out_specs=pl.BlockSpec((None, t, V_DIM), lambda b, h: (b, 0, h)),
        out_shape=jax.ShapeDtypeStruct((b, t, d), BF16),
        compiler_params=_cparams("parallel", "parallel"),
        name="diff_attention",
    )(qkv, qkv, qkv, lam_vecs, subln_g)


def _proj_res_kernel(a_ref, w_ref, x_ref, mt_ref, o_ref):
    acc = jnp.dot(a_ref[...], w_ref[...], preferred_element_type=F32)
    o_ref[...] = x_ref[...] + mt_ref[2:3, :] * acc


def _proj_res(a, w, xs, mt, tm, nct):
    b, t, d = xs.shape
    x_spec, mt_spec = _row_specs(tm, d, nct)
    return pl.pallas_call(
        _proj_res_kernel,
        grid=(b, t // tm),
        in_specs=[x_spec, _full((d, d)), x_spec, mt_spec],
        out_specs=x_spec,
        out_shape=jax.ShapeDtypeStruct((b, t, d), F32),
        compiler_params=_cparams("parallel", "parallel"),
        name="out_proj_residual",
    )(a, w, xs, mt)


def _s5_pack_kernel(x_ref, mt_ref, g_ref, o_ref, *, ncc):
    is_ctx = pl.program_id(0) < ncc
    for bi in range(x_ref.shape[0]):
        m = jnp.where(is_ctx, mt_ref[bi, 0], mt_ref[bi, 1])
        h = _rms_mod(x_ref[bi], g_ref[...], m[0:1, :], m[1:2, :])
        o_ref[:, bi, :] = h.T


def _s5_pack(xs, mt, g, ncc, bb):
    b, t, d = xs.shape
    tc = S5_CHUNK
    nc, nbb = t // tc, b // bb
    return pl.pallas_call(
        functools.partial(_s5_pack_kernel, ncc=ncc),
        grid=(nc, nbb),
        in_specs=[pl.BlockSpec((bb, tc, d), lambda c, h: (h, c, 0)),
                  pl.BlockSpec((bb, 2, 8, d), lambda c, h: (h, 0, 0, 0)),
                  pl.BlockSpec((1, d), lambda c, h: (0, 0))],
        out_specs=pl.BlockSpec((d, bb, tc), lambda c, h: (0, c * nbb + h, 0)),
        out_shape=jax.ShapeDtypeStruct((d, nc * b, tc), F32),
        compiler_params=_cparams("parallel", "parallel"),
        name="s5_norm_pack",
    )(xs, mt, g)


def _s5_group_kernel(u_ref, cbf_ref, cbb_ref, pwf_ref, pwb_ref, pa_ref, pb_ref, ba_ref, bb_ref,
                     qa_ref, qb_ref, ca_ref, cb_ref, dec_ref, dsk_ref, o_ref, m_scr, in_scr, st_scr, kf_scr, kb_scr,
                     *, nb, nc, ncc):
    tc = S5_CHUNK
    p = u_ref.shape[0]
    u = jnp.concatenate([u_ref[i] for i in range(p)], axis=1)
    ub = u.astype(BF16)

    m_in = jnp.concatenate(
        [(pa_ref[...] * ba_ref[i:i + 1, :] + pb_ref[...] * bb_ref[i:i + 1, :]).astype(BF16) for i in range(p)], axis=0)
    in_scr[...] = jnp.dot(ub, m_in, preferred_element_type=F32)

    a1 = dec_ref[0:1, :]
    a2 = dec_ref[1:2, :]

    def step(s, inc, lo):
        sl = slice(lo, lo + LANES)
        return a1[:, sl] * s + a2[:, sl] * pltpu.roll(s, LANES // 2, 1) + inc

    def scan_body(i, carry):
        s0, s1 = carry
        r0 = pl.multiple_of(i * nb, nb)
        c1 = jnp.where(i < ncc, ncc - 1 - i, nc + ncc - 1 - i)
        r1 = pl.multiple_of(c1 * nb, nb)
        st_scr[pl.ds(r0, nb), 0:LANES] = s0
        st_scr[pl.ds(r1, nb), LANES:2 * LANES] = s1
        s0 = step(s0, in_scr[pl.ds(r0, nb), 0:LANES], 0)
        s1 = step(s1, in_scr[pl.ds(r1, nb), LANES:2 * LANES], LANES)
        return s0, s1

    z = jnp.zeros((nb, LANES), F32)
    lax.fori_loop(0, nc, scan_body, (z, z))

    kf = _dot3(cbf_ref[...], pwf_ref[...], ((1,), (0,)))
    kb = _dot3(cbb_ref[...], pwb_ref[...], ((1,), (0,)))
    kf = kf + jnp.where(lax.broadcasted_iota(I32, kf.shape, 1) == 0, kb, 0.0)
    kf_scr[...] = kf
    kb_scr[...] = kb
    causal = lax.broadcasted_iota(I32, (tc, tc), 1) >= lax.broadcasted_iota(I32, (tc, tc), 0)

    def toeplitz_body(i, carry):
        for o in range(p):
            r = i * p + o
            f = jnp.broadcast_to(kf_scr[pl.ds(r, 1), :], (tc, tc))
            b = jnp.broadcast_to(kb_scr[pl.ds(r, 1), :], (tc, tc))
            f = pltpu.roll(f, 0, 1, stride=1, stride_axis=0)
            b = pltpu.roll(b, 0, 1, stride=1, stride_axis=0)
            m_scr[pl.ds(pl.multiple_of(i * tc, tc), tc), o * tc:(o + 1) * tc] = jnp.where(causal, f, b).astype(BF16)
        return carry

    lax.fori_loop(0, p, toeplitz_body, 0)

    m_out = jnp.concatenate(
        [(qa_ref[...] * ca_ref[:, o:o + 1] + qb_ref[...] * cb_ref[:, o:o + 1]).astype(BF16) for o in range(p)], axis=1)
    y = jnp.dot(ub, m_scr[...], preferred_element_type=F32)
    y = y + jnp.dot(st_scr[...].astype(BF16), m_out, preferred_element_type=F32)
    y = jax.nn.gelu(y + dsk_ref[...] * u)
    for o in range(p):
        o_ref[o] = y[:, o * tc:(o + 1) * tc]


def _s5_tables(lam_re, lam_im, log_dt, b_re, b_im, c_re, c_im, d_skip):
    tc = S5_CHUNK
    ndir, g, n = lam_re.shape
    p = b_re.shape[-1]
    dt = jnp.exp(log_dt)[..., None]
    mag = jnp.exp(lam_re * dt)
    ang = lam_im * dt
    ab_re, ab_im = mag * jnp.cos(ang), mag * jnp.sin(ang)
    den = lam_re * lam_re + lam_im * lam_im
    nr, ni = ab_re - 1.0, ab_im
    coef_re = (nr * lam_re + ni * lam_im) / den
    coef_im = (ni * lam_re - nr * lam_im) / den
    bb_re = coef_re[..., None] * b_re - coef_im[..., None] * b_im
    bb_im = coef_re[..., None] * b_im + coef_im[..., None] * b_re
    k = jnp.arange(tc + 1, dtype=F32)[None, None, :, None]
    pw_mag = jnp.exp(k * (lam_re * dt)[:, :, None, :])
    pw_re = pw_mag * jnp.cos(k * ang[:, :, None, :])
    pw_im = pw_mag * jnp.sin(k * ang[:, :, None, :])
    cb_re = (c_re[:, :, None, :, :] * bb_re.transpose(0, 1, 3, 2)[:, :, :, None, :]
             - c_im[:, :, None, :, :] * bb_im.transpose(0, 1, 3, 2)[:, :, :, None, :])
    cb_im = (c_re[:, :, None, :, :] * bb_im.transpose(0, 1, 3, 2)[:, :, :, None, :]
             + c_im[:, :, None, :, :] * bb_re.transpose(0, 1, 3, 2)[:, :, :, None, :])
    cb = jnp.concatenate([cb_re, -cb_im], axis=-1).reshape(ndir, g, p * p, 2 * n)
    lag_f = jnp.arange(tc)
    lag_b = jnp.concatenate([jnp.zeros((1,), jnp.int32), tc - jnp.arange(1, tc)])
    rows = lambda a: a.transpose(0, 2, 1)
    pwf = jnp.concatenate([rows(pw_re[0][:, lag_f]), rows(pw_im[0][:, lag_f])], axis=1)
    pwb = jnp.concatenate([rows(pw_re[1][:, lag_b]), rows(pw_im[1][:, lag_b])], axis=1)
    f_re, f_im = pw_re[0][:, tc - 1::-1][:, :tc], pw_im[0][:, tc - 1::-1][:, :tc]
    r_re, r_im = pw_re[1][:, :tc], pw_im[1][:, :tc]
    pa = jnp.concatenate([f_re, f_re, r_re, r_re], axis=-1)
    pb = jnp.concatenate([-f_im, f_im, -r_im, r_im], axis=-1)
    bt_re, bt_im = bb_re.transpose(0, 1, 3, 2), bb_im.transpose(0, 1, 3, 2)
    ba = jnp.concatenate([bt_re[0], bt_im[0], bt_re[1], bt_im[1]], axis=-1)
    bb = jnp.concatenate([bt_im[0], bt_re[0], bt_im[1], bt_re[1]], axis=-1)
    o_re0, o_im0 = rows(pw_re[0][:, 1:]), rows(pw_im[0][:, 1:])
    o_re1, o_im1 = rows(pw_re[1][:, tc:0:-1]), rows(pw_im[1][:, tc:0:-1])
    qa = jnp.concatenate([o_re0, -o_im0, o_re1, -o_im1], axis=1)
    qb = jnp.concatenate([-o_im0, -o_re0, -o_im1, -o_re1], axis=1)
    ct_re, ct_im = c_re.transpose(0, 1, 3, 2), c_im.transpose(0, 1, 3, 2)
    ca = jnp.concatenate([ct_re[0], ct_re[0], ct_re[1], ct_re[1]], axis=1)
    cbm = jnp.concatenate([ct_im[0], ct_im[0], ct_im[1], ct_im[1]], axis=1)
    a_re, a_im = pw_re[:, :, tc], pw_im[:, :, tc]
    a1 = jnp.concatenate([a_re[0], a_re[0], a_re[1], a_re[1]], axis=-1)
    a2 = jnp.concatenate([-a_im[0], a_im[0], -a_im[1], a_im[1]], axis=-1)
    decay = jnp.stack([a1, a2], axis=1)
    dsk = jnp.repeat(d_skip.reshape(g, 1, p), tc, axis=2)
    return cb[0], cb[1], pwf, pwb, pa, pb, ba, bb, qa, qb, ca, cbm, decay, dsk


def _s5_groups(act, tables, nb, ncc):
    d, rows, tc = act.shape
    g = tables[0].shape[0]
    p = d // g
    nc = rows // nb
    per_g = lambda a: pl.BlockSpec((None,) + a.shape[1:], lambda i: (i,) + (0,) * (a.ndim - 1))
    slab = pl.BlockSpec((p, rows, tc), lambda i: (i, 0, 0))
    sw = tables[4].shape[2]
    return pl.pallas_call(
        functools.partial(_s5_group_kernel, nb=nb, nc=nc, ncc=ncc),
        grid=(g,),
        in_specs=[slab] + [per_g(a) for a in tables],
        out_specs=slab,
        out_shape=jax.ShapeDtypeStruct((d, rows, tc), F32),
        scratch_shapes=[pltpu.VMEM((p * tc, p * tc), BF16), pltpu.VMEM((rows, sw), F32), pltpu.VMEM((rows, sw), F32),
                        pltpu.VMEM((p * p, tc), F32), pltpu.VMEM((p * p, tc), F32)],
        compiler_params=_cparams("parallel"),
        name="s5_group_mix",
    )(act, *tables)


def _s5_glu_kernel(y_ref, w1_ref, w2_ref, x_ref, mt_ref, o_ref, *, ncc):
    is_ctx = pl.program_id(0) < ncc
    nbb = y_ref.shape[1]
    a = jnp.concatenate([y_ref[:, bi, :].T.astype(BF16) for bi in range(nbb)], axis=0)
    z1 = jnp.dot(a, w1_ref[...], preferred_element_type=F32)
    z2 = jnp.dot(a, w2_ref[...], preferred_element_type=F32)
    z = z1 * jax.nn.sigmoid(z2)
    tc = y_ref.shape[2]
    for bi in range(nbb):
        gate = jnp.where(is_ctx, mt_ref[bi, 0], mt_ref[bi, 1])[2:3, :]
        o_ref[bi] = x_ref[bi] + gate * z[bi * tc:(bi + 1) * tc]


def _s5_glu(y, w1, w2, xs, mt, ncc, bb):
    b, t, d = xs.shape
    tc = S5_CHUNK
    nc, nbb = t // tc, b // bb
    x_spec = pl.BlockSpec((bb, tc, d), lambda c, h: (h, c, 0))
    return pl.pallas_call(
        functools.partial(_s5_glu_kernel, ncc=ncc),
        grid=(nc, nbb),
        in_specs=[pl.BlockSpec((d, bb, tc), lambda c, h: (0, c * nbb + h, 0)),
                  pl.BlockSpec((d, d), lambda c, h: (0, 0)), pl.BlockSpec((d, d), lambda c, h: (0, 0)),
                  x_spec, pl.BlockSpec((bb, 2, 8, d), lambda c, h: (h, 0, 0, 0))],
        out_specs=x_spec,
        out_shape=jax.ShapeDtypeStruct((b, t, d), F32),
        compiler_params=_cparams("parallel", "parallel"),
        name="s5_unpack_glu_residual",
    )(y, w1, w2, xs, mt)


def _router_kernel(x_ref, mt_ref, g_ref, wr_ref, br_ref, h_ref, aff_ref):
    h = _rms_mod(x_ref[...], g_ref[...], mt_ref[3:4, :], mt_ref[4:5, :])
    h_ref[...] = h.astype(BF16)
    logits = _dot3(wr_ref[...], h, ((1,), (1,))) + br_ref[...]
    p = jnp.exp(logits - jnp.max(logits, axis=0, keepdims=True))
    aff_ref[...] = p / jnp.sum(p, axis=0, keepdims=True)


def _router(xs, mt, g, w_rt, b_r, tm, nct):
    b, t, d = xs.shape
    e = w_rt.shape[0]
    x_spec, mt_spec = _row_specs(tm, d, nct)
    return pl.pallas_call(
        _router_kernel,
        grid=(b, t // tm),
        in_specs=[x_spec, mt_spec, _full((1, d)), _full((e, d)), _full((e, 1))],
        out_specs=[x_spec, pl.BlockSpec((None, e, tm), lambda b, r: (b, 0, r))],
        out_shape=[jax.ShapeDtypeStruct((b, t, d), BF16), jax.ShapeDtypeStruct((b, e, t), F32)],
        compiler_params=_cparams("parallel", "parallel"),
        name="norm_router",
    )(xs, mt, g, w_rt, b_r)


def _kth_largest_bits(bits_sets, caps):
    e = bits_sets[0].shape[0]

    def body(_, carry):
        out = []
        for bits, cap, (lo, hi) in zip(bits_sets, caps, carry):
            mid = lo + ((hi - lo) >> 1)
            cnt = jnp.sum(jnp.where(bits >= mid, 1.0, 0.0), axis=1, keepdims=True)
            ge = cnt >= float(cap)
            out.append((jnp.where(ge, mid, lo), jnp.where(ge, hi, mid)))
        return tuple(out)

    init = tuple((jnp.zeros((e, 1), I32), jnp.full((e, 1), 0x7F800001, I32)) for _ in bits_sets)
    return [lo for lo, _ in lax.fori_loop(0, 31, body, init)]


def _slots_from_threshold(bits, thr, cap, tri):
    capf = float(cap)
    gt = bits > thr
    eq = bits == thr
    need = capf - jnp.sum(jnp.where(gt, 1.0, 0.0), axis=1, keepdims=True)
    eq_rank = jnp.dot(jnp.where(eq, 1.0, 0.0).astype(BF16), tri, preferred_element_type=F32)
    sel = jnp.where(gt, 1.0, jnp.where(eq, jnp.where(eq_rank < need, 1.0, 0.0), 0.0))
    rank = jnp.dot(sel.astype(BF16), tri, preferred_element_type=F32)
    return jnp.where(sel > 0.5, rank.astype(I32), -1)


def _topk_kernel(aff_ref, tri_ref, slot_ref, *, lc, cap_c, cap_l):
    bits_c = pltpu.bitcast(aff_ref[:, 0:lc], I32)
    bits_l = pltpu.bitcast(aff_ref[:, lc:], I32)
    thr_c, thr_l = _kth_largest_bits((bits_c, bits_l), (cap_c, cap_l))
    slot_ref[:, 0:lc] = _slots_from_threshold(bits_c, thr_c, cap_c, tri_ref[0:lc, 0:lc])
    slot_ref[:, lc:] = _slots_from_threshold(bits_l, thr_l, cap_l, tri_ref[...])


def _topk(aff, tri, lc, cap_c, cap_l):
    b, e, t = aff.shape
    l = t - lc
    return pl.pallas_call(
        functools.partial(_topk_kernel, lc=lc, cap_c=cap_c, cap_l=cap_l),
        grid=(b,),
        in_specs=[pl.BlockSpec((None, e, t), lambda i: (i, 0, 0)), pl.BlockSpec((l, l), lambda i: (0, 0))],
        out_specs=pl.BlockSpec((None, e, t), lambda i: (i, 0, 0)),
        out_shape=jax.ShapeDtypeStruct((b, e, t), I32),
        compiler_params=_cparams("parallel"),
        name="expert_choice_topk",
    )(aff, tri)


def _expert_kernel(h_ref, slot_ref, aff_ref, wg_hbm, wu_hbm, wd_hbm, yl_ref, yc_ref,
                   wg_s, wu_s, wd_s, stage_g, stage_u, stage_d, sem, *, layer, lc, cap_c, cap_l):
    e, b = pl.program_id(0), pl.program_id(1)
    n_exp, n_chunk = pl.num_programs(0), pl.num_programs(1)
    rows_g, rows_d = stage_g.shape[1], stage_d.shape[1]
    cur = e % 2

    def chunk_copies(ex, c, slot):
        return (pltpu.make_async_copy(wg_hbm.at[layer, ex, pl.ds(c * rows_g, rows_g), :], stage_g.at[slot], sem.at[slot, 0]),
                pltpu.make_async_copy(wu_hbm.at[layer, ex, pl.ds(c * rows_g, rows_g), :], stage_u.at[slot], sem.at[slot, 1]),
                pltpu.make_async_copy(wd_hbm.at[layer, ex, pl.ds(c * rows_d, rows_d), :], stage_d.at[slot], sem.at[slot, 2]))

    def start_chunk(ex, c, slot):
        for cp in chunk_copies(ex, c, slot):
            cp.start()

    def finish_chunk(ex, c, slot, buf):
        for cp in chunk_copies(ex, c, slot):
            cp.wait()
        wg_s[buf, pl.ds(pl.multiple_of(c * rows_g, rows_g), rows_g), :] = stage_g[slot].astype(BF16)
        wu_s[buf, pl.ds(pl.multiple_of(c * rows_g, rows_g), rows_g), :] = stage_u[slot].astype(BF16)
        wd_s[buf, pl.ds(pl.multiple_of(c * rows_d, rows_d), rows_d), :] = stage_d[slot].astype(BF16)

    @pl.when((e == 0) & (b == 0))
    def _():
        start_chunk(0, 0, 0)

        def load(c, carry):
            @pl.when(c + 1 < n_chunk)
            def _():
                start_chunk(0, c + 1, (c + 1) % 2)
            finish_chunk(0, c, c % 2, 0)
            return carry

        lax.fori_loop(0, n_chunk, load, 0)

    @pl.when(e + 1 < n_exp)
    def _():
        @pl.when(b == 0)
        def _():
            start_chunk(e + 1, 0, 0)

        @pl.when(b + 1 < n_chunk)
        def _():
            start_chunk(e + 1, b + 1, (b + 1) % 2)

    wg_ref, wu_ref, wd_ref = wg_s.at[cur], wu_s.at[cur], wd_s.at[cur]
    slot = slot_ref[...]
    aff = aff_ref[...]

    def gather(lo, hi, cap):
        n = hi - lo
        hit = lax.broadcasted_iota(I32, (cap, n), 0) == slot[:, lo:hi]
        onehot = jnp.where(hit, 1.0, 0.0).astype(BF16)
        xin = jnp.dot(onehot, h_ref[lo:hi, :], preferred_element_type=F32).astype(BF16)
        w = jnp.sum(jnp.where(hit, aff[:, lo:hi], 0.0), axis=1, keepdims=True)
        return xin, w

    xl, wl = gather(lc, h_ref.shape[0], cap_l)
    xc, wc = gather(0, lc, cap_c)
    xin = jnp.concatenate([xl, xc], axis=0)
    gate = jnp.dot(xin, wg_ref[...], preferred_element_type=F32)
    up = jnp.dot(xin, wu_ref[...], preferred_element_type=F32)
    hid = (gate * jax.nn.sigmoid(gate) * up).astype(BF16)
    y = jnp.dot(hid, wd_ref[...], preferred_element_type=F32)
    yl_ref[...] = (y[0:cap_l] * wl).astype(BF16)
    yc_ref[...] = (y[cap_l:] * wc).astype(BF16)

    @pl.when(e + 1 < n_exp)
    def _():
        finish_chunk(e + 1, b, b % 2, 1 - cur)


def _experts(hb, slot4, aff4, wg, wu, wd, layer, lc, cap_c, cap_l):
    b, t, d = hb.shape
    _, e, _, f = wg.shape
    assert d % b == 0 and f % b == 0 and (d // b) % 16 == 0
    row = pl.BlockSpec((None, None, 1, t), lambda e_, b_: (b_, e_, 0, 0))
    hbm = pl.BlockSpec(memory_space=pl.ANY)
    return pl.pallas_call(
        functools.partial(_expert_kernel, layer=layer, lc=lc, cap_c=cap_c, cap_l=cap_l),
        grid=(e, b),
        in_specs=[pl.BlockSpec((None, t, d), lambda e_, b_: (b_, 0, 0)), row, row, hbm, hbm, hbm],
        out_specs=[pl.BlockSpec((None, None, cap_l, d), lambda e_, b_: (b_, e_, 0, 0)),
                   pl.BlockSpec((None, None, cap_c, d), lambda e_, b_: (b_, e_, 0, 0))],
        out_shape=[jax.ShapeDtypeStruct((b, e, cap_l, d), BF16), jax.ShapeDtypeStruct((b, e, cap_c, d), BF16)],
        scratch_shapes=[pltpu.VMEM((2, d, f), BF16), pltpu.VMEM((2, d, f), BF16), pltpu.VMEM((2, f, d), BF16),
                        pltpu.VMEM((2, d // b, f), F32), pltpu.VMEM((2, d // b, f), F32), pltpu.VMEM((2, f // b, d), F32),
                        pltpu.SemaphoreType.DMA((2, 3))],
        compiler_params=_cparams("arbitrary", "arbitrary"),
        name="expert_ffn",
    )(hb, slot4, aff4, wg, wu, wd)


def _combine_kernel(slot_ref, yl_ref, yc_ref, x_ref, mt_ref, o_ref, *, nct, cap_c, cap_l, n_exp):
    slot = slot_ref[...]
    tm = slot.shape[0]
    gate = mt_ref[5:6, :]

    @pl.when(pl.program_id(1) < nct)
    def _():
        lanes = lax.broadcasted_iota(I32, (tm, n_exp * cap_c), 1)
        hit = jnp.zeros((tm, n_exp * cap_c), F32)
        for e in range(n_exp):
            col = slot[:, e:e + 1]
            hit = hit + jnp.where((col >= 0) & (col + e * cap_c == lanes), 1.0, 0.0)
        moe = jnp.dot(hit.astype(BF16), yc_ref[...], preferred_element_type=F32)
        o_ref[...] = x_ref[...] + gate * moe

    @pl.when(pl.program_id(1) >= nct)
    def _():
        lanes = lax.broadcasted_iota(I32, (tm, cap_l), 1)
        parts = [jnp.where(slot[:, e:e + 1] == lanes, 1.0, 0.0).astype(BF16) for e in range(n_exp)]
        moe = jnp.dot(jnp.concatenate(parts, axis=1), yl_ref[...], preferred_element_type=F32)
        o_ref[...] = x_ref[...] + gate * moe


def _combine(slot_t, yl, yc, xs, mt, tm, nct, cap_c, cap_l):
    b, t, d = xs.shape
    e = slot_t.shape[2]
    x_spec, mt_spec = _row_specs(tm, d, nct)
    return pl.pallas_call(
        functools.partial(_combine_kernel, nct=nct, cap_c=cap_c, cap_l=cap_l, n_exp=e),
        grid=(b, t // tm),
        in_specs=[pl.BlockSpec((None, tm, e), lambda b, r: (b, r, 0)),
                  pl.BlockSpec((None, e * cap_l, d), lambda b, r: (b, 0, 0)),
                  pl.BlockSpec((None, e * cap_c, d), lambda b, r: (b, 0, 0)),
                  x_spec, mt_spec],
        out_specs=x_spec,
        out_shape=jax.ShapeDtypeStruct((b, t, d), F32),
        compiler_params=_cparams("parallel", "arbitrary"),
        name="moe_combine_residual",
    )(slot_t, yl, yc, xs, mt)


def _ec_moe(xs, mt, g, w_r, b_r, wg, wu, wd, layer, tri, tm, nct, lc):
    b, t, d = xs.shape
    e = w_r.shape[1]
    cap_c = 2 * lc // e
    cap_l = 2 * (t - lc) // e
    hb, aff = _router(xs, mt, g, w_r.T, b_r.reshape(e, 1), tm, nct)
    slot = _topk(aff, tri, lc, cap_c, cap_l)
    yl, yc = _experts(hb, slot.reshape(b, e, 1, t), aff.reshape(b, e, 1, t), wg, wu, wd, layer, lc, cap_c, cap_l)
    return _combine(jnp.swapaxes(slot, 1, 2), yl.reshape(b, e * cap_l, d), yc.reshape(b, e * cap_c, d),
                    xs, mt, tm, nct, cap_c, cap_l)


def _final_kernel(x_ref, g_ref, o_ref):
    x = x_ref[...]
    o_ref[...] = x * lax.rsqrt(jnp.mean(x * x, axis=-1, keepdims=True) + EPS) * g_ref[...]


def _final_norm(xs, g, tm, nct, l):
    b, t, d = xs.shape
    return pl.pallas_call(
        _final_kernel,
        grid=(b, l // tm),
        in_specs=[pl.BlockSpec((None, tm, d), lambda b, r: (b, r + nct, 0)), _full((1, d))],
        out_specs=pl.BlockSpec((None, tm, d), lambda b, r: (b, r, 0)),
        out_shape=jax.ShapeDtypeStruct((b, l, d), F32),
        compiler_params=_cparams("parallel", "parallel"),
        name="final_norm",
    )(xs, g)


def _rope_tables(lc, l):
    rows = l // GRID_W
    row = jnp.repeat(jnp.arange(rows, dtype=F32), GRID_W)
    col = jnp.tile(jnp.arange(GRID_W, dtype=F32), rows)
    inv = ROPE_BASE ** (-jnp.arange(ROPE_F, dtype=F32) / ROPE_F)
    ang = jnp.stack([row[:, None] * inv, col[:, None] * inv], axis=1)
    cos, sin = jnp.cos(ang), jnp.sin(ang)
    cos_l = jnp.concatenate([cos, cos], axis=-1).reshape(l, 4 * ROPE_F)
    sin_l = jnp.concatenate([-sin, sin], axis=-1).reshape(l, 4 * ROPE_F)
    cos_l = jnp.tile(cos_l, (1, LANES // (4 * ROPE_F)))
    sin_l = jnp.tile(sin_l, (1, LANES // (4 * ROPE_F)))
    cos_t = jnp.concatenate([jnp.ones((lc, LANES), F32), cos_l], axis=0)
    sin_t = jnp.concatenate([jnp.zeros((lc, LANES), F32), sin_l], axis=0)
    return cos_t, sin_t


def kernel(x, c, ctx, c_ctx, ada_w, ada_b, norm1_g, norm2_g, final_g, attn_w_qkv, attn_w_o, attn_lam_q1, attn_lam_k1, attn_lam_q2, attn_lam_k2, attn_subln_g, ssm_lam_re, ssm_lam_im, ssm_log_dt, ssm_b_re, ssm_b_im, ssm_c_re, ssm_c_im, ssm_d, ssm_w_glu1, ssm_w_glu2, moe_w_router, moe_b_router, moe_w_gate, moe_w_up, moe_w_down):
    b, l, d = x.shape
    lc = ctx.shape[1]
    depth = ada_w.shape[0]
    t = lc + l
    tm = 256 if lc % 256 == 0 and l % 256 == 0 else 128
    assert lc % tm == 0 and l % tm == 0 and lc % S5_CHUNK == 0 and l % S5_CHUNK == 0 and d == HEADS * V_DIM
    nct = lc // tm

    xs = jnp.concatenate([ctx, x], axis=1)
    rpad = -(b + 1) % 8
    cc = jnp.concatenate([c, c_ctx[None, :], jnp.zeros((rpad, d), F32)], axis=0)
    mods = _adaln(cc, ada_w, ada_b)
    cos_t, sin_t = _rope_tables(lc, l)
    tri = (jnp.arange(l)[:, None] < jnp.arange(l)[None, :]).astype(BF16)

    for i in range(depth):
        j = i // 2
        mod_l = mods[i, :b].reshape(b, 1, N_MOD, d)
        mod_c = jnp.broadcast_to(mods[i, b].reshape(1, 1, N_MOD, d), (b, 1, N_MOD, d))
        mt = jnp.pad(jnp.concatenate([mod_c, mod_l], axis=1), ((0, 0), (0, 0), (0, 8 - N_MOD), (0, 0)))
        g1 = norm1_g[i].reshape(1, d)
        g2 = norm2_g[i].reshape(1, d)
        if i % 2 == 0:
            lam_init = 0.8 - 0.6 * math.exp(-0.3 * i)
            qkv = _qkv_proj(xs, mt, g1, attn_w_qkv[j].astype(BF16), cos_t, sin_t, tm, nct)
            lam_vecs = jnp.zeros((8, LANES), F32).at[0:4, 0:HEAD_DIM].set(
                jnp.stack([attn_lam_q1[j], attn_lam_k1[j], attn_lam_q2[j], attn_lam_k2[j]]))
            o = _diff_attention(qkv, lam_vecs, attn_subln_g[j].reshape(1, V_DIM), lc, lam_init)
            xs = _proj_res(o, attn_w_o[j].astype(BF16), xs, mt, tm, nct)
        else:
            tables = _s5_tables(ssm_lam_re[j], ssm_lam_im[j], ssm_log_dt[j], ssm_b_re[j], ssm_b_im[j],
                                ssm_c_re[j], ssm_c_im[j], ssm_d[j])
            ncc = lc // S5_CHUNK
            bb = 8 if b % 8 == 0 else b
            act = _s5_pack(xs, mt, g1, ncc, bb)
            y = _s5_groups(act, tables, b, ncc)
            xs = _s5_glu(y, ssm_w_glu1[j].astype(BF16), ssm_w_glu2[j].astype(BF16), xs, mt, ncc, bb)
        xs = _ec_moe(xs, mt, g2, moe_w_router[i], moe_b_router[i], moe_w_gate, moe_w_up, moe_w_down, i,
                     tri, tm, nct, lc)
    return _final_norm(xs, final_g.reshape(1, d), tm, nct, l)
```

```python
import functools
import math

import jax
import jax.numpy as jnp
from jax import lax
from jax.experimental import pallas as pl
from jax.experimental.pallas import tpu as pltpu

F32 = jnp.float32
BF16 = jnp.bfloat16
I32 = jnp.int32

EPS = 1e-6
N_MOD = 6
HEADS = 8
HEAD_DIM = 64
V_DIM = 2 * HEAD_DIM
ROPE_BASE = 10000.0
ROPE_F = HEAD_DIM // 4
GRID_W = 64
LANES = 128
S5_CHUNK = LANES
VMEM_LIMIT = 56 * 1024 * 1024


def _cparams(*sem):
    return pltpu.CompilerParams(dimension_semantics=sem, vmem_limit_bytes=VMEM_LIMIT)


def _split(a):
    hi = a.astype(BF16)
    lo = (a - hi.astype(F32)).astype(BF16)
    return hi, lo


def _dot3(a, b, dims):
    ah, al = _split(a)
    bh, bl = _split(b)
    dn = (dims, ((), ()))
    d = functools.partial(lax.dot_general, dimension_numbers=dn, preferred_element_type=F32)
    return d(ah, bh) + (d(ah, bl) + d(al, bh))


def _rms_mod(x, g, shift, scale):
    ms = jnp.mean(x * x, axis=-1, keepdims=True)
    return (x * lax.rsqrt(ms + EPS) * g) * (1.0 + scale) + shift


def _mod_kernel(c_ref, w_ref, b_ref, o_ref):
    c = c_ref[...]
    s = c * jax.nn.sigmoid(c)
    o_ref[...] = _dot3(s, w_ref[...], ((1,), (0,))) + b_ref[...]


def _adaln(cc, ada_w, ada_b):
    depth, d, n = ada_w.shape
    r = cc.shape[0]
    tn = n // 4
    return pl.pallas_call(
        _mod_kernel,
        grid=(depth, n // tn),
        in_specs=[pl.BlockSpec((r, d), lambda i, j: (0, 0)),
                  pl.BlockSpec((None, d, tn), lambda i, j: (i, 0, j)),
                  pl.BlockSpec((None, 1, tn), lambda i, j: (i, 0, j))],
        out_specs=pl.BlockSpec((None, r, tn), lambda i, j: (i, 0, j)),
        out_shape=jax.ShapeDtypeStruct((depth, r, n), F32),
        compiler_params=_cparams("parallel", "parallel"),
        name="adaln_mod",
    )(cc, ada_w, ada_b.reshape(depth, 1, n))


def _row_specs(tm, d, nct):
    x_spec = pl.BlockSpec((None, tm, d), lambda b, r: (b, r, 0))
    mt_spec = pl.BlockSpec((None, None, 8, d), lambda b, r: (b, jnp.where(r < nct, 0, 1), 0, 0))
    return x_spec, mt_spec


def _full(shape):
    return pl.BlockSpec(shape, lambda b, r: (0,) * len(shape))


def _qkv_kernel(x_ref, mt_ref, g_ref, w_ref, cos_ref, sin_ref, o_ref, *, d, tn):
    h = _rms_mod(x_ref[...], g_ref[...], mt_ref[0:1, :], mt_ref[1:2, :]).astype(BF16)
    reps = tn // LANES
    cosw = jnp.concatenate([cos_ref[...]] * reps, axis=1)
    sinw = jnp.concatenate([sin_ref[...]] * reps, axis=1)
    lane = lax.broadcasted_iota(I32, (1, tn), 1)
    first_half = (lane % (2 * ROPE_F)) < ROPE_F
    for j in range(3 * d // tn):
        acc = jnp.dot(h, w_ref[:, j * tn:(j + 1) * tn], preferred_element_type=F32)
        if j * tn < 2 * d:
            partner = jnp.where(first_half, pltpu.roll(acc, tn - ROPE_F, 1), pltpu.roll(acc, ROPE_F, 1))
            acc = acc * cosw + partner * sinw
        if j * tn < d:
            acc = acc * (HEAD_DIM ** -0.5 * math.log2(math.e))
        o_ref[:, j * tn:(j + 1) * tn] = acc.astype(BF16)


def _qkv_proj(xs, mt, g, w, cos_t, sin_t, tm, nct):
    b, t, d = xs.shape
    tn = 512
    x_spec, mt_spec = _row_specs(tm, d, nct)
    return pl.pallas_call(
        functools.partial(_qkv_kernel, d=d, tn=tn),
        grid=(b, t // tm),
        in_specs=[x_spec, mt_spec, _full((1, d)), _full((d, 3 * d)),
                  pl.BlockSpec((tm, LANES), lambda b, r: (r, 0)),
                  pl.BlockSpec((tm, LANES), lambda b, r: (r, 0))],
        out_specs=pl.BlockSpec((None, tm, 3 * d), lambda b, r: (b, r, 0)),
        out_shape=jax.ShapeDtypeStruct((b, t, 3 * d), BF16),
        compiler_params=_cparams("parallel", "parallel"),
        name="norm_qkv_rope",
    )(xs, mt, g, w, cos_t, sin_t)


def _attn_kernel(q_ref, k_ref, v_ref, lam_ref, g_ref, o_ref, *, lc, tq, lam_init):
    lv = lam_ref[...]
    e1 = jnp.exp(jnp.sum(lv[0:1] * lv[1:2], axis=1, keepdims=True))
    e2 = jnp.exp(jnp.sum(lv[2:3] * lv[3:4], axis=1, keepdims=True))
    lam = e1 - e2 + lam_init
    g = g_ref[...] * (1.0 - lam_init)
    lane = lax.broadcasted_iota(I32, (1, V_DIM), 1)
    comp = (lane < HEAD_DIM, lane >= HEAD_DIM)
    nt = (((1,), (1,)), ((), ()))
    t = k_ref.shape[0]
    ones_col = jnp.where(lax.broadcasted_iota(I32, (t, V_DIM), 1) == 0, 1.0, 0.0).astype(BF16)
    v1 = jnp.concatenate([v_ref[...], ones_col], axis=1)
    kk = k_ref[...]

    def block(row0, nrows, nk):
        q = q_ref[pl.ds(row0, nrows), :]
        outs = []
        for c in range(2):
            qc = jnp.where(comp[c], q, jnp.zeros_like(q))
            s = lax.dot_general(qc, kk[0:nk], nt, preferred_element_type=F32)
            p = jnp.exp2(s - jnp.max(s, axis=-1, keepdims=True))
            acc = jnp.dot(p.astype(BF16), v1[0:nk], preferred_element_type=F32)
            outs.append(acc[:, 0:V_DIM] / acc[:, V_DIM:V_DIM + 1])
        o = outs[0] - lam * outs[1]
        o = o * lax.rsqrt(jnp.mean(o * o, axis=-1, keepdims=True) + EPS) * g
        o_ref[pl.ds(row0, nrows), :] = o.astype(BF16)

    tc = min(tq, lc)
    for r in range(lc // tc):
        block(r * tc, tc, lc)

    def body(i, carry):
        block(pl.multiple_of(lc + i * tq, tq), tq, t)
        return carry

    lax.fori_loop(0, (t - lc) // tq, body, 0, unroll=8)


def _diff_attention(qkv, lam_vecs, subln_g, lc, lam_init):
    b, t, d3 = qkv.shape
    d = d3 // 3
    tq = 256 if (t - lc) % 256 == 0 and lc % 256 == 0 else 128
    slab = lambda off: pl.BlockSpec((None, t, V_DIM), lambda b, h: (b, 0, off + h))
    return pl.pallas_call(
        functools.partial(_attn_kernel, lc=lc, tq=tq, lam_init=lam_init),
        grid=(b, HEADS),
        in_specs=[slab(0), slab(HEADS), slab(2 * HEADS),
                  pl.BlockSpec((8, LANES), lambda b, h: (0, 0)),
                  pl.BlockSpec((1, V_DIM), lambda b, h: (0, 0))],
        out_specs=pl.BlockSpec((None, t, V_DIM), lambda b, h: (b, 0, h)),
        out_shape=jax.ShapeDtypeStruct((b, t, d), BF16),
        compiler_params=_cparams("parallel", "parallel"),
        name="diff_attention",
    )(qkv, qkv, qkv, lam_vecs, subln_g)


def _proj_res_kernel(a_ref, w_ref, x_ref, mt_ref, o_ref):
    acc = jnp.dot(a_ref[...], w_ref[...], preferred_element_type=F32)
    o_ref[...] = x_ref[...] + mt_ref[2:3, :] * acc


def _proj_res(a, w, xs, mt, tm, nct):
    b, t, d = xs.shape
    x_spec, mt_spec = _row_specs(tm, d, nct)
    return pl.pallas_call(
        _proj_res_kernel,
        grid=(b, t // tm),
        in_specs=[x_spec, _full((d, d)), x_spec, mt_spec],
        out_specs=x_spec,
        out_shape=jax.ShapeDtypeStruct((b, t, d), F32),
        compiler_params=_cparams("parallel", "parallel"),
        name="out_proj_residual",
    )(a, w, xs, mt)


def _s5_pack_kernel(x_ref, mt_ref, g_ref, o_ref, *, ncc):
    is_ctx = pl.program_id(0) < ncc
    for bi in range(x_ref.shape[0]):
        m = jnp.where(is_ctx, mt_ref[bi, 0], mt_ref[bi, 1])
        h = _rms_mod(x_ref[bi], g_ref[...], m[0:1, :], m[1:2, :])
        o_ref[:, bi, :] = h.T


def _s5_pack(xs, mt, g, ncc, bb):
    b, t, d = xs.shape
    tc = S5_CHUNK
    nc, nbb = t // tc, b // bb
    return pl.pallas_call(
        functools.partial(_s5_pack_kernel, ncc=ncc),
        grid=(nc, nbb),
        in_specs=[pl.BlockSpec((bb, tc, d), lambda c, h: (h, c, 0)),
                  pl.BlockSpec((bb, 2, 8, d), lambda c, h: (h, 0, 0, 0)),
                  pl.BlockSpec((1, d), lambda c, h: (0, 0))],
        out_specs=pl.BlockSpec((d, bb, tc), lambda c, h: (0, c * nbb + h, 0)),
        out_shape=jax.ShapeDtypeStruct((d, nc * b, tc), F32),
        compiler_params=_cparams("parallel", "parallel"),
        name="s5_norm_pack",
    )(xs, mt, g)


def _s5_group_kernel(u_ref, cbf_ref, cbb_ref, pwf_ref, pwb_ref, pa_ref, pb_ref, ba_ref, bb_ref,
                     qa_ref, qb_ref, ca_ref, cb_ref, dec_ref, dsk_ref, o_ref, *, nb, nc, ncc):
    tc = S5_CHUNK
    p = u_ref.shape[0]
    u = jnp.concatenate([u_ref[i] for i in range(p)], axis=1)
    ub = u.astype(BF16)

    m_in = jnp.concatenate(
        [(pa_ref[...] * ba_ref[i:i + 1, :] + pb_ref[...] * bb_ref[i:i + 1, :]).astype(BF16) for i in range(p)], axis=0)
    s_in = jnp.dot(ub, m_in, preferred_element_type=F32)

    a1 = dec_ref[0:1, :]
    a2 = dec_ref[1:2, :]

    def step(s, c, lo):
        sl = slice(lo, lo + LANES)
        return a1[:, sl] * s + a2[:, sl] * pltpu.roll(s, LANES // 2, 1) + s_in[c * nb:(c + 1) * nb, sl]

    fwd, bwd = [None] * nc, [None] * nc
    s0 = s1 = jnp.zeros((nb, LANES), F32)
    for i in range(nc):
        c1 = ncc - 1 - i if i < ncc else nc + ncc - 1 - i
        fwd[i], bwd[c1] = s0, s1
        s0, s1 = step(s0, i, 0), step(s1, c1, LANES)
    states = jnp.concatenate([jnp.concatenate(fwd, axis=0), jnp.concatenate(bwd, axis=0)], axis=1).astype(BF16)

    m_out = jnp.concatenate(
        [(qa_ref[...] * ca_ref[:, o:o + 1] + qb_ref[...] * cb_ref[:, o:o + 1]).astype(BF16) for o in range(p)], axis=1)
    y = jnp.dot(states, m_out, preferred_element_type=F32) + dsk_ref[...] * u

    kf = _dot3(cbf_ref[...], pwf_ref[...], ((1,), (0,)))
    kb = _dot3(cbb_ref[...], pwb_ref[...], ((1,), (0,)))
    kf = kf + jnp.where(lax.broadcasted_iota(I32, kf.shape, 1) == 0, kb, 0.0)
    causal = lax.broadcasted_iota(I32, (tc, tc), 1) >= lax.broadcasted_iota(I32, (tc, tc), 0)

    def toeplitz(r):
        f = pltpu.roll(jnp.broadcast_to(kf[r:r + 1, :], (tc, tc)), 0, 1, stride=1, stride_axis=0)
        b = pltpu.roll(jnp.broadcast_to(kb[r:r + 1, :], (tc, tc)), 0, 1, stride=1, stride_axis=0)
        return jnp.where(causal, f, b).astype(BF16)

    pair = 2 * tc
    for j in range(p // 2):
        m_rows = jnp.concatenate(
            [jnp.concatenate([toeplitz(i * p + o) for o in range(p)], axis=1) for i in (2 * j, 2 * j + 1)], axis=0)
        y = y + jnp.dot(ub[:, j * pair:(j + 1) * pair], m_rows, preferred_element_type=F32)
    y = jax.nn.gelu(y)
    for o in range(p):
        o_ref[o] = y[:, o * tc:(o + 1) * tc]


def _s5_tables(lam_re, lam_im, log_dt, b_re, b_im, c_re, c_im, d_skip):
    tc = S5_CHUNK
    ndir, g, n = lam_re.shape
    p = b_re.shape[-1]
    dt = jnp.exp(log_dt)[..., None]
    mag = jnp.exp(lam_re * dt)
    ang = lam_im * dt
    ab_re, ab_im = mag * jnp.cos(ang), mag * jnp.sin(ang)
    den = lam_re * lam_re + lam_im * lam_im
    nr, ni = ab_re - 1.0, ab_im
    coef_re = (nr * lam_re + ni * lam_im) / den
    coef_im = (ni * lam_re - nr * lam_im) / den
    bb_re = coef_re[..., None] * b_re - coef_im[..., None] * b_im
    bb_im = coef_re[..., None] * b_im + coef_im[..., None] * b_re
    k = jnp.arange(tc + 1, dtype=F32)[None, None, :, None]
    pw_mag = jnp.exp(k * (lam_re * dt)[:, :, None, :])
    pw_re = pw_mag * jnp.cos(k * ang[:, :, None, :])
    pw_im = pw_mag * jnp.sin(k * ang[:, :, None, :])
    cb_re = (c_re[:, :, None, :, :] * bb_re.transpose(0, 1, 3, 2)[:, :, :, None, :]
             - c_im[:, :, None, :, :] * bb_im.transpose(0, 1, 3, 2)[:, :, :, None, :])
    cb_im = (c_re[:, :, None, :, :] * bb_im.transpose(0, 1, 3, 2)[:, :, :, None, :]
             + c_im[:, :, None, :, :] * bb_re.transpose(0, 1, 3, 2)[:, :, :, None, :])
    cb = jnp.concatenate([cb_re, -cb_im], axis=-1).reshape(ndir, g, p * p, 2 * n)
    lag_f = jnp.arange(tc)
    lag_b = jnp.concatenate([jnp.zeros((1,), jnp.int32), tc - jnp.arange(1, tc)])
    rows = lambda a: a.transpose(0, 2, 1)
    pwf = jnp.concatenate([rows(pw_re[0][:, lag_f]), rows(pw_im[0][:, lag_f])], axis=1)
    pwb = jnp.concatenate([rows(pw_re[1][:, lag_b]), rows(pw_im[1][:, lag_b])], axis=1)
    f_re, f_im = pw_re[0][:, tc - 1::-1][:, :tc], pw_im[0][:, tc - 1::-1][:, :tc]
    r_re, r_im = pw_re[1][:, :tc], pw_im[1][:, :tc]
    pa = jnp.concatenate([f_re, f_re, r_re, r_re], axis=-1)
    pb = jnp.concatenate([-f_im, f_im, -r_im, r_im], axis=-1)
    bt_re, bt_im = bb_re.transpose(0, 1, 3, 2), bb_im.transpose(0, 1, 3, 2)
    ba = jnp.concatenate([bt_re[0], bt_im[0], bt_re[1], bt_im[1]], axis=-1)
    bb = jnp.concatenate([bt_im[0], bt_re[0], bt_im[1], bt_re[1]], axis=-1)
    o_re0, o_im0 = rows(pw_re[0][:, 1:]), rows(pw_im[0][:, 1:])
    o_re1, o_im1 = rows(pw_re[1][:, tc:0:-1]), rows(pw_im[1][:, tc:0:-1])
    qa = jnp.concatenate([o_re0, -o_im0, o_re1, -o_im1], axis=1)
    qb = jnp.concatenate([-o_im0, -o_re0, -o_im1, -o_re1], axis=1)
    ct_re, ct_im = c_re.transpose(0, 1, 3, 2), c_im.transpose(0, 1, 3, 2)
    ca = jnp.concatenate([ct_re[0], ct_re[0], ct_re[1], ct_re[1]], axis=1)
    cbm = jnp.concatenate([ct_im[0], ct_im[0], ct_im[1], ct_im[1]], axis=1)
    a_re, a_im = pw_re[:, :, tc], pw_im[:, :, tc]
    a1 = jnp.concatenate([a_re[0], a_re[0], a_re[1], a_re[1]], axis=-1)
    a2 = jnp.concatenate([-a_im[0], a_im[0], -a_im[1], a_im[1]], axis=-1)
    decay = jnp.stack([a1, a2], axis=1)
    dsk = jnp.repeat(d_skip.reshape(g, 1, p), tc, axis=2)
    return cb[0], cb[1], pwf, pwb, pa, pb, ba, bb, qa, qb, ca, cbm, decay, dsk


def _s5_groups(act, tables, nb, ncc):
    d, rows, tc = act.shape
    g = tables[0].shape[0]
    p = d // g
    nc = rows // nb
    per_g = lambda a: pl.BlockSpec((None,) + a.shape[1:], lambda i: (i,) + (0,) * (a.ndim - 1))
    slab = pl.BlockSpec((p, rows, tc), lambda i: (i, 0, 0))
    sw = tables[4].shape[2]
    return pl.pallas_call(
        functools.partial(_s5_group_kernel, nb=nb, nc=nc, ncc=ncc),
        grid=(g,),
        in_specs=[slab] + [per_g(a) for a in tables],
        out_specs=slab,
        out_shape=jax.ShapeDtypeStruct((d, rows, tc), F32),
        compiler_params=_cparams("parallel"),
        name="s5_group_mix",
    )(act, *tables)


def _s5_glu_kernel(y_ref, w1_ref, w2_ref, x_ref, mt_ref, o_ref, *, ncc):
    is_ctx = pl.program_id(0) < ncc
    nbb = y_ref.shape[1]
    a = jnp.concatenate([y_ref[:, bi, :].T.astype(BF16) for bi in range(nbb)], axis=0)
    z1 = jnp.dot(a, w1_ref[...], preferred_element_type=F32)
    z2 = jnp.dot(a, w2_ref[...], preferred_element_type=F32)
    z = z1 * jax.nn.sigmoid(z2)
    tc = y_ref.shape[2]
    for bi in range(nbb):
        gate = jnp.where(is_ctx, mt_ref[bi, 0], mt_ref[bi, 1])[2:3, :]
        o_ref[bi] = x_ref[bi] + gate * z[bi * tc:(bi + 1) * tc]


def _s5_glu(y, w1, w2, xs, mt, ncc, bb):
    b, t, d = xs.shape
    tc = S5_CHUNK
    nc, nbb = t // tc, b // bb
    x_spec = pl.BlockSpec((bb, tc, d), lambda c, h: (h, c, 0))
    return pl.pallas_call(
        functools.partial(_s5_glu_kernel, ncc=ncc),
        grid=(nc, nbb),
        in_specs=[pl.BlockSpec((d, bb, tc), lambda c, h: (0, c * nbb + h, 0)),
                  pl.BlockSpec((d, d), lambda c, h: (0, 0)), pl.BlockSpec((d, d), lambda c, h: (0, 0)),
                  x_spec, pl.BlockSpec((bb, 2, 8, d), lambda c, h: (h, 0, 0, 0))],
        out_specs=x_spec,
        out_shape=jax.ShapeDtypeStruct((b, t, d), F32),
        compiler_params=_cparams("parallel", "parallel"),
        name="s5_unpack_glu_residual",
    )(y, w1, w2, xs, mt)


def _router_kernel(x_ref, mt_ref, g_ref, wr_ref, br_ref, h_ref, aff_ref):
    h = _rms_mod(x_ref[...], g_ref[...], mt_ref[3:4, :], mt_ref[4:5, :])
    h_ref[...] = h.astype(BF16)
    logits = _dot3(wr_ref[...], h, ((1,), (1,))) + br_ref[...]
    p = jnp.exp(logits - jnp.max(logits, axis=0, keepdims=True))
    aff_ref[...] = p / jnp.sum(p, axis=0, keepdims=True)


def _router(xs, mt, g, w_rt, b_r, tm, nct):
    b, t, d = xs.shape
    e = w_rt.shape[0]
    x_spec, mt_spec = _row_specs(tm, d, nct)
    return pl.pallas_call(
        _router_kernel,
        grid=(b, t // tm),
        in_specs=[x_spec, mt_spec, _full((1, d)), _full((e, d)), _full((e, 1))],
        out_specs=[x_spec, pl.BlockSpec((None, e, tm), lambda b, r: (b, 0, r))],
        out_shape=[jax.ShapeDtypeStruct((b, t, d), BF16), jax.ShapeDtypeStruct((b, e, t), F32)],
        compiler_params=_cparams("parallel", "parallel"),
        name="norm_router",
    )(xs, mt, g, w_rt, b_r)


def _kth_largest_bits(bits_sets, caps):
    e = bits_sets[0].shape[0]

    def body(_, carry):
        out = []
        for bits, cap, (lo, hi) in zip(bits_sets, caps, carry):
            mid = lo + ((hi - lo) >> 1)
            cnt = jnp.sum(jnp.where(bits >= mid, 1.0, 0.0), axis=1, keepdims=True)
            ge = cnt >= float(cap)
            out.append((jnp.where(ge, mid, lo), jnp.where(ge, hi, mid)))
        return tuple(out)

    init = tuple((jnp.zeros((e, 1), I32), jnp.full((e, 1), 0x7F800001, I32)) for _ in bits_sets)
    return [lo for lo, _ in lax.fori_loop(0, 31, body, init)]


def _slots_from_threshold(bits, thr, cap, tri):
    capf = float(cap)
    gt = bits > thr
    eq = bits == thr
    need = capf - jnp.sum(jnp.where(gt, 1.0, 0.0), axis=1, keepdims=True)
    eq_rank = jnp.dot(jnp.where(eq, 1.0, 0.0).astype(BF16), tri, preferred_element_type=F32)
    sel = jnp.where(gt, 1.0, jnp.where(eq, jnp.where(eq_rank < need, 1.0, 0.0), 0.0))
    rank = jnp.dot(sel.astype(BF16), tri, preferred_element_type=F32)
    return jnp.where(sel > 0.5, rank.astype(I32), -1)


def _topk_kernel(aff_ref, tri_ref, slot_ref, *, lc, cap_c, cap_l):
    bits_c = pltpu.bitcast(aff_ref[:, 0:lc], I32)
    bits_l = pltpu.bitcast(aff_ref[:, lc:], I32)
    thr_c, thr_l = _kth_largest_bits((bits_c, bits_l), (cap_c, cap_l))
    slot_ref[:, 0:lc] = _slots_from_threshold(bits_c, thr_c, cap_c, tri_ref[0:lc, 0:lc])
    slot_ref[:, lc:] = _slots_from_threshold(bits_l, thr_l, cap_l, tri_ref[...])


def _topk(aff, tri, lc, cap_c, cap_l):
    b, e, t = aff.shape
    l = t - lc
    return pl.pallas_call(
        functools.partial(_topk_kernel, lc=lc, cap_c=cap_c, cap_l=cap_l),
        grid=(b,),
        in_specs=[pl.BlockSpec((None, e, t), lambda i: (i, 0, 0)), pl.BlockSpec((l, l), lambda i: (0, 0))],
        out_specs=pl.BlockSpec((None, e, t), lambda i: (i, 0, 0)),
        out_shape=jax.ShapeDtypeStruct((b, e, t), I32),
        compiler_params=_cparams("parallel"),
        name="expert_choice_topk",
    )(aff, tri)


def _expert_kernel(h_ref, slot_ref, aff_ref, wg_hbm, wu_hbm, wd_hbm, yl_ref, yc_ref,
                   wg_s, wu_s, wd_s, stage_g, stage_u, stage_d, sem, *, layer, lc, cap_c, cap_l):
    e, b = pl.program_id(0), pl.program_id(1)
    n_exp, n_chunk = pl.num_programs(0), pl.num_programs(1)
    rows_g, rows_d = stage_g.shape[1], stage_d.shape[1]
    cur = e % 2

    def chunk_copies(ex, c, slot):
        return (pltpu.make_async_copy(wg_hbm.at[layer, ex, pl.ds(c * rows_g, rows_g), :], stage_g.at[slot], sem.at[slot, 0]),
                pltpu.make_async_copy(wu_hbm.at[layer, ex, pl.ds(c * rows_g, rows_g), :], stage_u.at[slot], sem.at[slot, 1]),
                pltpu.make_async_copy(wd_hbm.at[layer, ex, pl.ds(c * rows_d, rows_d), :], stage_d.at[slot], sem.at[slot, 2]))

    def start_chunk(ex, c, slot):
        for cp in chunk_copies(ex, c, slot):
            cp.start()

    def finish_chunk(ex, c, slot, buf):
        for cp in chunk_copies(ex, c, slot):
            cp.wait()
        wg_s[buf, pl.ds(pl.multiple_of(c * rows_g, rows_g), rows_g), :] = stage_g[slot].astype(BF16)
        wu_s[buf, pl.ds(pl.multiple_of(c * rows_g, rows_g), rows_g), :] = stage_u[slot].astype(BF16)
        wd_s[buf, pl.ds(pl.multiple_of(c * rows_d, rows_d), rows_d), :] = stage_d[slot].astype(BF16)

    @pl.when((e == 0) & (b == 0))
    def _():
        start_chunk(0, 0, 0)

        def load(c, carry):
            @pl.when(c + 1 < n_chunk)
            def _():
                start_chunk(0, c + 1, (c + 1) % 2)
            finish_chunk(0, c, c % 2, 0)
            return carry

        lax.fori_loop(0, n_chunk, load, 0)

    @pl.when(e + 1 < n_exp)
    def _():
        @pl.when(b == 0)
        def _():
            start_chunk(e + 1, 0, 0)

        @pl.when(b + 1 < n_chunk)
        def _():
            start_chunk(e + 1, b + 1, (b + 1) % 2)

    wg_ref, wu_ref, wd_ref = wg_s.at[cur], wu_s.at[cur], wd_s.at[cur]
    slot = slot_ref[...]
    aff = aff_ref[...]

    def gather(lo, hi, cap):
        n = hi - lo
        hit = lax.broadcasted_iota(I32, (cap, n), 0) == slot[:, lo:hi]
        onehot = jnp.where(hit, 1.0, 0.0).astype(BF16)
        xin = jnp.dot(onehot, h_ref[lo:hi, :], preferred_element_type=F32).astype(BF16)
        w = jnp.sum(jnp.where(hit, aff[:, lo:hi], 0.0), axis=1, keepdims=True)
        return xin, w

    xl, wl = gather(lc, h_ref.shape[0], cap_l)
    xc, wc = gather(0, lc, cap_c)
    xin = jnp.concatenate([xl, xc], axis=0)
    gate = jnp.dot(xin, wg_ref[...], preferred_element_type=F32)
    up = jnp.dot(xin, wu_ref[...], preferred_element_type=F32)
    hid = (gate * jax.nn.sigmoid(gate) * up).astype(BF16)
    y = jnp.dot(hid, wd_ref[...], preferred_element_type=F32)
    yl_ref[...] = (y[0:cap_l] * wl).astype(BF16)
    yc_ref[...] = (y[cap_l:] * wc).astype(BF16)

    @pl.when(e + 1 < n_exp)
    def _():
        finish_chunk(e + 1, b, b % 2, 1 - cur)


def _experts(hb, slot4, aff4, wg, wu, wd, layer, lc, cap_c, cap_l):
    b, t, d = hb.shape
    _, e, _, f = wg.shape
    assert d % b == 0 and f % b == 0 and (d // b) % 16 == 0
    row = pl.BlockSpec((None, None, 1, t), lambda e_, b_: (b_, e_, 0, 0))
    hbm = pl.BlockSpec(memory_space=pl.ANY)
    return pl.pallas_call(
        functools.partial(_expert_kernel, layer=layer, lc=lc, cap_c=cap_c, cap_l=cap_l),
        grid=(e, b),
        in_specs=[pl.BlockSpec((None, t, d), lambda e_, b_: (b_, 0, 0)), row, row, hbm, hbm, hbm],
        out_specs=[pl.BlockSpec((None, None, cap_l, d), lambda e_, b_: (b_, e_, 0, 0)),
                   pl.BlockSpec((None, None, cap_c, d), lambda e_, b_: (b_, e_, 0, 0))],
        out_shape=[jax.ShapeDtypeStruct((b, e, cap_l, d), BF16), jax.ShapeDtypeStruct((b, e, cap_c, d), BF16)],
        scratch_shapes=[pltpu.VMEM((2, d, f), BF16), pltpu.VMEM((2, d, f), BF16), pltpu.VMEM((2, f, d), BF16),
                        pltpu.VMEM((2, d // b, f), F32), pltpu.VMEM((2, d // b, f), F32), pltpu.VMEM((2, f // b, d), F32),
                        pltpu.SemaphoreType.DMA((2, 3))],
        compiler_params=_cparams("arbitrary", "arbitrary"),
        name="expert_ffn",
    )(hb, slot4, aff4, wg, wu, wd)


def _combine_kernel(slot_ref, yl_ref, yc_ref, x_ref, mt_ref, o_ref, *, nct, cap_c, cap_l, n_exp):
    slot = slot_ref[...]
    tm = slot.shape[0]
    gate = mt_ref[5:6, :]

    @pl.when(pl.program_id(1) < nct)
    def _():
        lanes = lax.broadcasted_iota(I32, (tm, n_exp * cap_c), 1)
        hit = jnp.zeros((tm, n_exp * cap_c), F32)
        for e in range(n_exp):
            col = slot[:, e:e + 1]
            hit = hit + jnp.where((col >= 0) & (col + e * cap_c == lanes), 1.0, 0.0)
        moe = jnp.dot(hit.astype(BF16), yc_ref[...], preferred_element_type=F32)
        o_ref[...] = x_ref[...] + gate * moe

    @pl.when(pl.program_id(1) >= nct)
    def _():
        lanes = lax.broadcasted_iota(I32, (tm, cap_l), 1)
        parts = [jnp.where(slot[:, e:e + 1] == lanes, 1.0, 0.0).astype(BF16) for e in range(n_exp)]
        moe = jnp.dot(jnp.concatenate(parts, axis=1), yl_ref[...], preferred_element_type=F32)
        o_ref[...] = x_ref[...] + gate * moe


def _combine(slot_t, yl, yc, xs, mt, tm, nct, cap_c, cap_l):
    b, t, d = xs.shape
    e = slot_t.shape[2]
    x_spec, mt_spec = _row_specs(tm, d, nct)
    return pl.pallas_call(
        functools.partial(_combine_kernel, nct=nct, cap_c=cap_c, cap_l=cap_l, n_exp=e),
        grid=(b, t // tm),
        in_specs=[pl.BlockSpec((None, tm, e), lambda b, r: (b, r, 0)),
                  pl.BlockSpec((None, e * cap_l, d), lambda b, r: (b, 0, 0)),
                  pl.BlockSpec((None, e * cap_c, d), lambda b, r: (b, 0, 0)),
                  x_spec, mt_spec],
        out_specs=x_spec,
        out_shape=jax.ShapeDtypeStruct((b, t, d), F32),
        compiler_params=_cparams("parallel", "arbitrary"),
        name="moe_combine_residual",
    )(slot_t, yl, yc, xs, mt)


def _combine_final_kernel(slot_ref, yl_ref, x_ref, mt_ref, g_ref, o_ref, *, cap_l, n_exp):
    slot = slot_ref[...]
    lanes = lax.broadcasted_iota(I32, (slot.shape[0], cap_l), 1)
    parts = [jnp.where(slot[:, e:e + 1] == lanes, 1.0, 0.0).astype(BF16) for e in range(n_exp)]
    moe = jnp.dot(jnp.concatenate(parts, axis=1), yl_ref[...], preferred_element_type=F32)
    x = x_ref[...] + mt_ref[5:6, :] * moe
    o_ref[...] = x * lax.rsqrt(jnp.mean(x * x, axis=-1, keepdims=True) + EPS) * g_ref[...]


def _combine_final(slot_t, yl, xs, mt, final_g, tm, nct, cap_l):
    b, t, d = xs.shape
    e = slot_t.shape[2]
    return pl.pallas_call(
        functools.partial(_combine_final_kernel, cap_l=cap_l, n_exp=e),
        grid=(b, t // tm - nct),
        in_specs=[pl.BlockSpec((None, tm, e), lambda b, r: (b, r + nct, 0)),
                  pl.BlockSpec((None, e * cap_l, d), lambda b, r: (b, 0, 0)),
                  pl.BlockSpec((None, tm, d), lambda b, r: (b, r + nct, 0)),
                  pl.BlockSpec((None, None, 8, d), lambda b, r: (b, 1, 0, 0)),
                  pl.BlockSpec((1, d), lambda b, r: (0, 0))],
        out_specs=pl.BlockSpec((None, tm, d), lambda b, r: (b, r, 0)),
        out_shape=jax.ShapeDtypeStruct((b, t - nct * tm, d), F32),
        compiler_params=_cparams("parallel", "arbitrary"),
        name="moe_combine_final_norm",
    )(slot_t, yl, xs, mt, final_g)


def _ec_moe(xs, mt, g, w_r, b_r, wg, wu, wd, layer, tri, tm, nct, lc, final_g=None):
    b, t, d = xs.shape
    e = w_r.shape[1]
    cap_c = 2 * lc // e
    cap_l = 2 * (t - lc) // e
    hb, aff = _router(xs, mt, g, w_r.T, b_r.reshape(e, 1), tm, nct)
    slot = _topk(aff, tri, lc, cap_c, cap_l)
    yl, yc = _experts(hb, slot.reshape(b, e, 1, t), aff.reshape(b, e, 1, t), wg, wu, wd, layer, lc, cap_c, cap_l)
    if final_g is not None:
        return _combine_final(jnp.swapaxes(slot, 1, 2), yl.reshape(b, e * cap_l, d), xs, mt, final_g, tm, nct, cap_l)
    return _combine(jnp.swapaxes(slot, 1, 2), yl.reshape(b, e * cap_l, d), yc.reshape(b, e * cap_c, d),
                    xs, mt, tm, nct, cap_c, cap_l)


def _rope_tables(lc, l):
    rows = l // GRID_W
    row = jnp.repeat(jnp.arange(rows, dtype=F32), GRID_W)
    col = jnp.tile(jnp.arange(GRID_W, dtype=F32), rows)
    inv = ROPE_BASE ** (-jnp.arange(ROPE_F, dtype=F32) / ROPE_F)
    ang = jnp.stack([row[:, None] * inv, col[:, None] * inv], axis=1)
    cos, sin = jnp.cos(ang), jnp.sin(ang)
    cos_l = jnp.concatenate([cos, cos], axis=-1).reshape(l, 4 * ROPE_F)
    sin_l = jnp.concatenate([-sin, sin], axis=-1).reshape(l, 4 * ROPE_F)
    cos_l = jnp.tile(cos_l, (1, LANES // (4 * ROPE_F)))
    sin_l = jnp.tile(sin_l, (1, LANES // (4 * ROPE_F)))
    cos_t = jnp.concatenate([jnp.ones((lc, LANES), F32), cos_l], axis=0)
    sin_t = jnp.concatenate([jnp.zeros((lc, LANES), F32), sin_l], axis=0)
    return cos_t, sin_t


def kernel(x, c, ctx, c_ctx, ada_w, ada_b, norm1_g, norm2_g, final_g, attn_w_qkv, attn_w_o, attn_lam_q1, attn_lam_k1, attn_lam_q2, attn_lam_k2, attn_subln_g, ssm_lam_re, ssm_lam_im, ssm_log_dt, ssm_b_re, ssm_b_im, ssm_c_re, ssm_c_im, ssm_d, ssm_w_glu1, ssm_w_glu2, moe_w_router, moe_b_router, moe_w_gate, moe_w_up, moe_w_down):
    b, l, d = x.shape
    lc = ctx.shape[1]
    depth = ada_w.shape[0]
    t = lc + l
    tm = 256 if lc % 256 == 0 and l % 256 == 0 else 128
    assert lc % tm == 0 and l % tm == 0 and lc % S5_CHUNK == 0 and l % S5_CHUNK == 0 and d == HEADS * V_DIM
    nct = lc // tm

    xs = jnp.concatenate([ctx, x], axis=1)
    rpad = -(b + 1) % 8
    cc = jnp.concatenate([c, c_ctx[None, :], jnp.zeros((rpad, d), F32)], axis=0)
    mods = _adaln(cc, ada_w, ada_b)
    cos_t, sin_t = _rope_tables(lc, l)
    tri = (jnp.arange(l)[:, None] < jnp.arange(l)[None, :]).astype(BF16)

    for i in range(depth):
        j = i // 2
        mod_l = mods[i, :b].reshape(b, 1, N_MOD, d)
        mod_c = jnp.broadcast_to(mods[i, b].reshape(1, 1, N_MOD, d), (b, 1, N_MOD, d))
        mt = jnp.pad(jnp.concatenate([mod_c, mod_l], axis=1), ((0, 0), (0, 0), (0, 8 - N_MOD), (0, 0)))
        g1 = norm1_g[i].reshape(1, d)
        g2 = norm2_g[i].reshape(1, d)
        if i % 2 == 0:
            lam_init = 0.8 - 0.6 * math.exp(-0.3 * i)
            qkv = _qkv_proj(xs, mt, g1, attn_w_qkv[j].astype(BF16), cos_t, sin_t, tm, nct)
            lam_vecs = jnp.zeros((8, LANES), F32).at[0:4, 0:HEAD_DIM].set(
                jnp.stack([attn_lam_q1[j], attn_lam_k1[j], attn_lam_q2[j], attn_lam_k2[j]]))
            o = _diff_attention(qkv, lam_vecs, attn_subln_g[j].reshape(1, V_DIM), lc, lam_init)
            xs = _proj_res(o, attn_w_o[j].astype(BF16), xs, mt, tm, nct)
        else:
            tables = _s5_tables(ssm_lam_re[j], ssm_lam_im[j], ssm_log_dt[j], ssm_b_re[j], ssm_b_im[j],
                                ssm_c_re[j], ssm_c_im[j], ssm_d[j])
            ncc = lc // S5_CHUNK
            bb = 8 if b % 8 == 0 else b
            act = _s5_pack(xs, mt, g1, ncc, bb)
            y = _s5_groups(act, tables, b, ncc)
            xs = _s5_glu(y, ssm_w_glu1[j].astype(BF16), ssm_w_glu2[j].astype(BF16), xs, mt, ncc, bb)
        xs = _ec_moe(xs, mt, g2, moe_w_router[i], moe_b_router[i], moe_w_gate, moe_w_up, moe_w_down, i,
                     tri, tm, nct, lc, final_g=final_g.reshape(1, d) if i == depth - 1 else None)
    return xs
```

```python
import functools
import math

import jax
import jax.numpy as jnp
from jax import lax
from jax.experimental import pallas as pl
from jax.experimental.pallas import tpu as pltpu

F32 = jnp.float32
BF16 = jnp.bfloat16
I32 = jnp.int32

EPS = 1e-6
N_MOD = 6
HEADS = 8
HEAD_DIM = 64
V_DIM = 2 * HEAD_DIM
ROPE_BASE = 10000.0
ROPE_F = HEAD_DIM // 4
GRID_W = 64
LANES = 128
S5_CHUNK = LANES
VMEM_LIMIT = 56 * 1024 * 1024


def _cparams(*sem):
    return pltpu.CompilerParams(dimension_semantics=sem, vmem_limit_bytes=VMEM_LIMIT)


def _split(a):
    hi = a.astype(BF16)
    lo = (a - hi.astype(F32)).astype(BF16)
    return hi, lo


def _dot3(a, b, dims):
    ah, al = _split(a)
    bh, bl = _split(b)
    dn = (dims, ((), ()))
    d = functools.partial(lax.dot_general, dimension_numbers=dn, preferred_element_type=F32)
    return d(ah, bh) + (d(ah, bl) + d(al, bh))


def _rms_mod(x, g, shift, scale):
    ms = jnp.mean(x * x, axis=-1, keepdims=True)
    return (x * lax.rsqrt(ms + EPS) * g) * (1.0 + scale) + shift


def _route(x, m, g, wr, br):
    h = _rms_mod(x, g, m[3:4, :], m[4:5, :])
    logits = _dot3(wr, h, ((1,), (1,))) + br
    p = jnp.exp(logits - jnp.max(logits, axis=0, keepdims=True))
    return h.astype(BF16), p / jnp.sum(p, axis=0, keepdims=True)


def _mod_kernel(c_ref, w_ref, b_ref, o_ref):
    c = c_ref[...]
    s = c * jax.nn.sigmoid(c)
    o_ref[...] = _dot3(s, w_ref[...], ((1,), (0,))) + b_ref[...]


def _adaln(cc, ada_w, ada_b):
    depth, d, n = ada_w.shape
    r = cc.shape[0]
    tn = n // 4
    return pl.pallas_call(
        _mod_kernel,
        grid=(depth, n // tn),
        in_specs=[pl.BlockSpec((r, d), lambda i, j: (0, 0)),
                  pl.BlockSpec((None, d, tn), lambda i, j: (i, 0, j)),
                  pl.BlockSpec((None, 1, tn), lambda i, j: (i, 0, j))],
        out_specs=pl.BlockSpec((None, r, tn), lambda i, j: (i, 0, j)),
        out_shape=jax.ShapeDtypeStruct((depth, r, n), F32),
        compiler_params=_cparams("parallel", "parallel"),
        name="adaln_mod",
    )(cc, ada_w, ada_b.reshape(depth, 1, n))


def _row_specs(tm, d, nct):
    x_spec = pl.BlockSpec((None, tm, d), lambda b, r: (b, r, 0))
    mt_spec = pl.BlockSpec((None, None, 8, d), lambda b, r: (b, jnp.where(r < nct, 0, 1), 0, 0))
    return x_spec, mt_spec


def _full(shape):
    return pl.BlockSpec(shape, lambda b, r: (0,) * len(shape))


def _qkv_kernel(x_ref, mt_ref, g_ref, w_ref, cos_ref, sin_ref, o_ref, *, d, tn):
    h = _rms_mod(x_ref[...], g_ref[...], mt_ref[0:1, :], mt_ref[1:2, :]).astype(BF16)
    reps = tn // LANES
    cosw = jnp.concatenate([cos_ref[...]] * reps, axis=1)
    sinw = jnp.concatenate([sin_ref[...]] * reps, axis=1)
    lane = lax.broadcasted_iota(I32, (1, tn), 1)
    first_half = (lane % (2 * ROPE_F)) < ROPE_F
    for j in range(3 * d // tn):
        acc = jnp.dot(h, w_ref[:, j * tn:(j + 1) * tn], preferred_element_type=F32)
        if j * tn < 2 * d:
            partner = jnp.where(first_half, pltpu.roll(acc, tn - ROPE_F, 1), pltpu.roll(acc, ROPE_F, 1))
            acc = acc * cosw + partner * sinw
        if j * tn < d:
            acc = acc * (HEAD_DIM ** -0.5 * math.log2(math.e))
        o_ref[:, j * tn:(j + 1) * tn] = acc.astype(BF16)


def _qkv_proj(xs, mt, g, w, cos_t, sin_t, tm, nct):
    b, t, d = xs.shape
    tn = 512
    x_spec, mt_spec = _row_specs(tm, d, nct)
    return pl.pallas_call(
        functools.partial(_qkv_kernel, d=d, tn=tn),
        grid=(b, t // tm),
        in_specs=[x_spec, mt_spec, _full((1, d)), _full((d, 3 * d)),
                  pl.BlockSpec((tm, LANES), lambda b, r: (r, 0)),
                  pl.BlockSpec((tm, LANES), lambda b, r: (r, 0))],
        out_specs=pl.BlockSpec((None, tm, 3 * d), lambda b, r: (b, r, 0)),
        out_shape=jax.ShapeDtypeStruct((b, t, 3 * d), BF16),
        compiler_params=_cparams("parallel", "parallel"),
        name="norm_qkv_rope",
    )(xs, mt, g, w, cos_t, sin_t)


def _attn_kernel(q_ref, k_ref, v_ref, lam_ref, g_ref, o_ref, *, lc, tq, lam_init):
    lv = lam_ref[...]
    e1 = jnp.exp(jnp.sum(lv[0:1] * lv[1:2], axis=1, keepdims=True))
    e2 = jnp.exp(jnp.sum(lv[2:3] * lv[3:4], axis=1, keepdims=True))
    lam = e1 - e2 + lam_init
    g = g_ref[...] * (1.0 - lam_init)
    lane = lax.broadcasted_iota(I32, (1, V_DIM), 1)
    comp = (lane < HEAD_DIM, lane >= HEAD_DIM)
    nt = (((1,), (1,)), ((), ()))
    t = k_ref.shape[0]
    ones_col = jnp.where(lax.broadcasted_iota(I32, (t, V_DIM), 1) == 0, 1.0, 0.0).astype(BF16)
    v1 = jnp.concatenate([v_ref[...], ones_col], axis=1)
    kk = k_ref[...]

    def block(row0, nrows, nk):
        q = q_ref[pl.ds(row0, nrows), :]
        outs = []
        for c in range(2):
            qc = jnp.where(comp[c], q, jnp.zeros_like(q))
            s = lax.dot_general(qc, kk[0:nk], nt, preferred_element_type=F32)
            p = jnp.exp2(s - jnp.max(s, axis=-1, keepdims=True))
            acc = jnp.dot(p.astype(BF16), v1[0:nk], preferred_element_type=F32)
            outs.append(acc[:, 0:V_DIM] / acc[:, V_DIM:V_DIM + 1])
        o = outs[0] - lam * outs[1]
        o = o * lax.rsqrt(jnp.mean(o * o, axis=-1, keepdims=True) + EPS) * g
        o_ref[pl.ds(row0, nrows), :] = o.astype(BF16)

    tc = min(tq, lc)
    for r in range(lc // tc):
        block(r * tc, tc, lc)

    def body(i, carry):
        block(pl.multiple_of(lc + i * tq, tq), tq, t)
        return carry

    lax.fori_loop(0, (t - lc) // tq, body, 0, unroll=8)


def _diff_attention(qkv, lam_vecs, subln_g, lc, lam_init):
    b, t, d3 = qkv.shape
    d = d3 // 3
    tq = 256 if (t - lc) % 256 == 0 and lc % 256 == 0 else 128
    slab = lambda off: pl.BlockSpec((None, t, V_DIM), lambda b, h: (b, 0, off + h))
    return pl.pallas_call(
        functools.partial(_attn_kernel, lc=lc, tq=tq, lam_init=lam_init),
        grid=(b, HEADS),
        in_specs=[slab(0), slab(HEADS), slab(2 * HEADS),
                  pl.BlockSpec((8, LANES), lambda b, h: (0, 0)),
                  pl.BlockSpec((1, V_DIM), lambda b, h: (0, 0))],
        out_specs=pl.BlockSpec((None, t, V_DIM), lambda b, h: (b, 0, h)),
        out_shape=jax.ShapeDtypeStruct((b, t, d), BF16),
        compiler_params=_cparams("parallel", "parallel"),
        name="diff_attention",
    )(qkv, qkv, qkv, lam_vecs, subln_g)


def _proj_res_kernel(a_ref, w_ref, x_ref, mt_ref, g2_ref, wr_ref, br_ref, o_ref, h_ref, aff_ref):
    acc = jnp.dot(a_ref[...], w_ref[...], preferred_element_type=F32)
    x = x_ref[...] + mt_ref[2:3, :] * acc
    o_ref[...] = x
    h_ref[...], aff_ref[...] = _route(x, mt_ref[...], g2_ref[...], wr_ref[...], br_ref[...])


def _proj_res(a, w, xs, mt, g2, w_rt, b_r, tm, nct):
    b, t, d = xs.shape
    e = w_rt.shape[0]
    x_spec, mt_spec = _row_specs(tm, d, nct)
    return pl.pallas_call(
        _proj_res_kernel,
        grid=(b, t // tm),
        in_specs=[x_spec, _full((d, d)), x_spec, mt_spec, _full((1, d)), _full((e, d)), _full((e, 1))],
        out_specs=[x_spec, x_spec, pl.BlockSpec((None, e, tm), lambda b, r: (b, 0, r))],
        out_shape=[jax.ShapeDtypeStruct((b, t, d), F32), jax.ShapeDtypeStruct((b, t, d), BF16),
                   jax.ShapeDtypeStruct((b, e, t), F32)],
        compiler_params=_cparams("parallel", "parallel"),
        name="out_proj_residual_route",
    )(a, w, xs, mt, g2, w_rt, b_r)


def _s5_pack_kernel(x_ref, mt_ref, g_ref, o_ref, *, ncc):
    is_ctx = pl.program_id(0) < ncc
    for bi in range(x_ref.shape[0]):
        m = jnp.where(is_ctx, mt_ref[bi, 0], mt_ref[bi, 1])
        h = _rms_mod(x_ref[bi], g_ref[...], m[0:1, :], m[1:2, :])
        o_ref[:, bi, :] = h.T


def _s5_pack(xs, mt, g, ncc, bb):
    b, t, d = xs.shape
    tc = S5_CHUNK
    nc, nbb = t // tc, b // bb
    return pl.pallas_call(
        functools.partial(_s5_pack_kernel, ncc=ncc),
        grid=(nc, nbb),
        in_specs=[pl.BlockSpec((bb, tc, d), lambda c, h: (h, c, 0)),
                  pl.BlockSpec((bb, 2, 8, d), lambda c, h: (h, 0, 0, 0)),
                  pl.BlockSpec((1, d), lambda c, h: (0, 0))],
        out_specs=pl.BlockSpec((d, bb, tc), lambda c, h: (0, c * nbb + h, 0)),
        out_shape=jax.ShapeDtypeStruct((d, nc * b, tc), F32),
        compiler_params=_cparams("parallel", "parallel"),
        name="s5_norm_pack",
    )(xs, mt, g)


def _s5_group_kernel(u_ref, cbf_ref, cbb_ref, pwf_ref, pwb_ref, pa_ref, pb_ref, ba_ref, bb_ref,
                     qa_ref, qb_ref, ca_ref, cb_ref, dec_ref, dsk_ref, o_ref, *, nb, nc, ncc):
    tc = S5_CHUNK
    p = u_ref.shape[0]
    u = jnp.concatenate([u_ref[i] for i in range(p)], axis=1)
    ub = u.astype(BF16)

    m_in = jnp.concatenate(
        [(pa_ref[...] * ba_ref[i:i + 1, :] + pb_ref[...] * bb_ref[i:i + 1, :]).astype(BF16) for i in range(p)], axis=0)
    s_in = jnp.dot(ub, m_in, preferred_element_type=F32)

    a1 = dec_ref[0:1, :]
    a2 = dec_ref[1:2, :]

    def step(s, c, lo):
        sl = slice(lo, lo + LANES)
        return a1[:, sl] * s + a2[:, sl] * pltpu.roll(s, LANES // 2, 1) + s_in[c * nb:(c + 1) * nb, sl]

    fwd, bwd = [None] * nc, [None] * nc
    s0 = s1 = jnp.zeros((nb, LANES), F32)
    for i in range(nc):
        c1 = ncc - 1 - i if i < ncc else nc + ncc - 1 - i
        fwd[i], bwd[c1] = s0, s1
        s0, s1 = step(s0, i, 0), step(s1, c1, LANES)
    states = jnp.concatenate([jnp.concatenate(fwd, axis=0), jnp.concatenate(bwd, axis=0)], axis=1).astype(BF16)

    m_out = jnp.concatenate(
        [(qa_ref[...] * ca_ref[:, o:o + 1] + qb_ref[...] * cb_ref[:, o:o + 1]).astype(BF16) for o in range(p)], axis=1)
    y = jnp.dot(states, m_out, preferred_element_type=F32) + dsk_ref[...] * u

    kf = _dot3(cbf_ref[...], pwf_ref[...], ((1,), (0,)))
    kb = _dot3(cbb_ref[...], pwb_ref[...], ((1,), (0,)))
    kf = kf + jnp.where(lax.broadcasted_iota(I32, kf.shape, 1) == 0, kb, 0.0)
    causal = lax.broadcasted_iota(I32, (tc, tc), 1) >= lax.broadcasted_iota(I32, (tc, tc), 0)

    def toeplitz(r):
        f = pltpu.roll(jnp.broadcast_to(kf[r:r + 1, :], (tc, tc)), 0, 1, stride=1, stride_axis=0)
        b = pltpu.roll(jnp.broadcast_to(kb[r:r + 1, :], (tc, tc)), 0, 1, stride=1, stride_axis=0)
        return jnp.where(causal, f, b).astype(BF16)

    pair = 2 * tc
    for j in range(p // 2):
        m_rows = jnp.concatenate(
            [jnp.concatenate([toeplitz(i * p + o) for o in range(p)], axis=1) for i in (2 * j, 2 * j + 1)], axis=0)
        y = y + jnp.dot(ub[:, j * pair:(j + 1) * pair], m_rows, preferred_element_type=F32)
    y = jax.nn.gelu(y)
    for o in range(p):
        o_ref[o] = y[:, o * tc:(o + 1) * tc]


def _s5_tables(lam_re, lam_im, log_dt, b_re, b_im, c_re, c_im, d_skip):
    tc = S5_CHUNK
    ndir, g, n = lam_re.shape
    p = b_re.shape[-1]
    dt = jnp.exp(log_dt)[..., None]
    mag = jnp.exp(lam_re * dt)
    ang = lam_im * dt
    ab_re, ab_im = mag * jnp.cos(ang), mag * jnp.sin(ang)
    den = lam_re * lam_re + lam_im * lam_im
    nr, ni = ab_re - 1.0, ab_im
    coef_re = (nr * lam_re + ni * lam_im) / den
    coef_im = (ni * lam_re - nr * lam_im) / den
    bb_re = coef_re[..., None] * b_re - coef_im[..., None] * b_im
    bb_im = coef_re[..., None] * b_im + coef_im[..., None] * b_re
    k = jnp.arange(tc + 1, dtype=F32)[None, None, :, None]
    pw_mag = jnp.exp(k * (lam_re * dt)[:, :, None, :])
    pw_re = pw_mag * jnp.cos(k * ang[:, :, None, :])
    pw_im = pw_mag * jnp.sin(k * ang[:, :, None, :])
    cb_re = (c_re[:, :, None, :, :] * bb_re.transpose(0, 1, 3, 2)[:, :, :, None, :]
             - c_im[:, :, None, :, :] * bb_im.transpose(0, 1, 3, 2)[:, :, :, None, :])
    cb_im = (c_re[:, :, None, :, :] * bb_im.transpose(0, 1, 3, 2)[:, :, :, None, :]
             + c_im[:, :, None, :, :] * bb_re.transpose(0, 1, 3, 2)[:, :, :, None, :])
    cb = jnp.concatenate([cb_re, -cb_im], axis=-1).reshape(ndir, g, p * p, 2 * n)
    lag_f = jnp.arange(tc)
    lag_b = jnp.concatenate([jnp.zeros((1,), jnp.int32), tc - jnp.arange(1, tc)])
    rows = lambda a: a.transpose(0, 2, 1)
    pwf = jnp.concatenate([rows(pw_re[0][:, lag_f]), rows(pw_im[0][:, lag_f])], axis=1)
    pwb = jnp.concatenate([rows(pw_re[1][:, lag_b]), rows(pw_im[1][:, lag_b])], axis=1)
    f_re, f_im = pw_re[0][:, tc - 1::-1][:, :tc], pw_im[0][:, tc - 1::-1][:, :tc]
    r_re, r_im = pw_re[1][:, :tc], pw_im[1][:, :tc]
    pa = jnp.concatenate([f_re, f_re, r_re, r_re], axis=-1)
    pb = jnp.concatenate([-f_im, f_im, -r_im, r_im], axis=-1)
    bt_re, bt_im = bb_re.transpose(0, 1, 3, 2), bb_im.transpose(0, 1, 3, 2)
    ba = jnp.concatenate([bt_re[0], bt_im[0], bt_re[1], bt_im[1]], axis=-1)
    bb = jnp.concatenate([bt_im[0], bt_re[0], bt_im[1], bt_re[1]], axis=-1)
    o_re0, o_im0 = rows(pw_re[0][:, 1:]), rows(pw_im[0][:, 1:])
    o_re1, o_im1 = rows(pw_re[1][:, tc:0:-1]), rows(pw_im[1][:, tc:0:-1])
    qa = jnp.concatenate([o_re0, -o_im0, o_re1, -o_im1], axis=1)
    qb = jnp.concatenate([-o_im0, -o_re0, -o_im1, -o_re1], axis=1)
    ct_re, ct_im = c_re.transpose(0, 1, 3, 2), c_im.transpose(0, 1, 3, 2)
    ca = jnp.concatenate([ct_re[0], ct_re[0], ct_re[1], ct_re[1]], axis=1)
    cbm = jnp.concatenate([ct_im[0], ct_im[0], ct_im[1], ct_im[1]], axis=1)
    a_re, a_im = pw_re[:, :, tc], pw_im[:, :, tc]
    a1 = jnp.concatenate([a_re[0], a_re[0], a_re[1], a_re[1]], axis=-1)
    a2 = jnp.concatenate([-a_im[0], a_im[0], -a_im[1], a_im[1]], axis=-1)
    decay = jnp.stack([a1, a2], axis=1)
    dsk = jnp.repeat(d_skip.reshape(g, 1, p), tc, axis=2)
    return cb[0], cb[1], pwf, pwb, pa, pb, ba, bb, qa, qb, ca, cbm, decay, dsk


def _s5_groups(act, tables, nb, ncc):
    d, rows, tc = act.shape
    g = tables[0].shape[0]
    p = d // g
    nc = rows // nb
    per_g = lambda a: pl.BlockSpec((None,) + a.shape[1:], lambda i: (i,) + (0,) * (a.ndim - 1))
    slab = pl.BlockSpec((p, rows, tc), lambda i: (i, 0, 0))
    sw = tables[4].shape[2]
    return pl.pallas_call(
        functools.partial(_s5_group_kernel, nb=nb, nc=nc, ncc=ncc),
        grid=(g,),
        in_specs=[slab] + [per_g(a) for a in tables],
        out_specs=slab,
        out_shape=jax.ShapeDtypeStruct((d, rows, tc), F32),
        compiler_params=_cparams("parallel"),
        name="s5_group_mix",
    )(act, *tables)


def _s5_glu_kernel(y_ref, w1_ref, w2_ref, x_ref, mt_ref, g2_ref, wr_ref, br_ref, o_ref, h_ref, aff_ref, *, ncc):
    is_ctx = pl.program_id(0) < ncc
    nbb = y_ref.shape[1]
    a = jnp.concatenate([y_ref[:, bi, :].T.astype(BF16) for bi in range(nbb)], axis=0)
    z1 = jnp.dot(a, w1_ref[...], preferred_element_type=F32)
    z2 = jnp.dot(a, w2_ref[...], preferred_element_type=F32)
    z = z1 * jax.nn.sigmoid(z2)
    tc = y_ref.shape[2]
    for bi in range(nbb):
        m = jnp.where(is_ctx, mt_ref[bi, 0], mt_ref[bi, 1])
        x = x_ref[bi] + m[2:3, :] * z[bi * tc:(bi + 1) * tc]
        o_ref[bi] = x
        h_ref[bi], aff_ref[bi] = _route(x, m, g2_ref[...], wr_ref[...], br_ref[...])


def _s5_glu(y, w1, w2, xs, mt, g2, w_rt, b_r, ncc, bb):
    b, t, d = xs.shape
    e = w_rt.shape[0]
    tc = S5_CHUNK
    nc, nbb = t // tc, b // bb
    x_spec = pl.BlockSpec((bb, tc, d), lambda c, h: (h, c, 0))
    const = lambda shape: pl.BlockSpec(shape, lambda c, h: (0,) * len(shape))
    return pl.pallas_call(
        functools.partial(_s5_glu_kernel, ncc=ncc),
        grid=(nc, nbb),
        in_specs=[pl.BlockSpec((d, bb, tc), lambda c, h: (0, c * nbb + h, 0)), const((d, d)), const((d, d)),
                  x_spec, pl.BlockSpec((bb, 2, 8, d), lambda c, h: (h, 0, 0, 0)),
                  const((1, d)), const((e, d)), const((e, 1))],
        out_specs=[x_spec, x_spec, pl.BlockSpec((bb, e, tc), lambda c, h: (h, 0, c))],
        out_shape=[jax.ShapeDtypeStruct((b, t, d), F32), jax.ShapeDtypeStruct((b, t, d), BF16),
                   jax.ShapeDtypeStruct((b, e, t), F32)],
        compiler_params=_cparams("parallel", "parallel"),
        name="s5_unpack_glu_residual_route",
    )(y, w1, w2, xs, mt, g2, w_rt, b_r)


def _kth_largest_bits(bits_sets, caps):
    e = bits_sets[0].shape[0]

    def body(_, carry):
        out = []
        for bits, cap, (lo, hi) in zip(bits_sets, caps, carry):
            mid = lo + ((hi - lo) >> 1)
            cnt = jnp.sum(jnp.where(bits >= mid, 1.0, 0.0), axis=1, keepdims=True)
            ge = cnt >= float(cap)
            out.append((jnp.where(ge, mid, lo), jnp.where(ge, hi, mid)))
        return tuple(out)

    init = tuple((jnp.zeros((e, 1), I32), jnp.full((e, 1), 0x7F800001, I32)) for _ in bits_sets)
    return [lo for lo, _ in lax.fori_loop(0, 31, body, init)]


def _slots_from_threshold(bits, thr, cap, tri):
    capf = float(cap)
    gt = bits > thr
    eq = bits == thr
    need = capf - jnp.sum(jnp.where(gt, 1.0, 0.0), axis=1, keepdims=True)
    eq_rank = jnp.dot(jnp.where(eq, 1.0, 0.0).astype(BF16), tri, preferred_element_type=F32)
    sel = jnp.where(gt, 1.0, jnp.where(eq, jnp.where(eq_rank < need, 1.0, 0.0), 0.0))
    rank = jnp.dot(sel.astype(BF16), tri, preferred_element_type=F32)
    return jnp.where(sel > 0.5, rank.astype(I32), -1)


def _topk_kernel(aff_ref, tri_ref, slot_ref, *, lc, cap_c, cap_l):
    bits_c = pltpu.bitcast(aff_ref[:, 0:lc], I32)
    bits_l = pltpu.bitcast(aff_ref[:, lc:], I32)
    thr_c, thr_l = _kth_largest_bits((bits_c, bits_l), (cap_c, cap_l))
    slot_ref[:, 0:lc] = _slots_from_threshold(bits_c, thr_c, cap_c, tri_ref[0:lc, 0:lc])
    slot_ref[:, lc:] = _slots_from_threshold(bits_l, thr_l, cap_l, tri_ref[...])


def _topk(aff, tri, lc, cap_c, cap_l):
    b, e, t = aff.shape
    l = t - lc
    return pl.pallas_call(
        functools.partial(_topk_kernel, lc=lc, cap_c=cap_c, cap_l=cap_l),
        grid=(b,),
        in_specs=[pl.BlockSpec((None, e, t), lambda i: (i, 0, 0)), pl.BlockSpec((l, l), lambda i: (0, 0))],
        out_specs=pl.BlockSpec((None, e, t), lambda i: (i, 0, 0)),
        out_shape=jax.ShapeDtypeStruct((b, e, t), I32),
        compiler_params=_cparams("parallel"),
        name="expert_choice_topk",
    )(aff, tri)


def _expert_kernel(h_ref, slot_ref, aff_ref, wg_hbm, wu_hbm, wd_hbm, yl_ref, yc_ref,
                   wg_s, wu_s, wd_s, stage_g, stage_u, stage_d, sem, *, layer, lc, cap_c, cap_l):
    e, b = pl.program_id(0), pl.program_id(1)
    n_exp, n_chunk = pl.num_programs(0), pl.num_programs(1)
    rows_g, rows_d = stage_g.shape[1], stage_d.shape[1]
    cur = e % 2

    def chunk_copies(ex, c, slot):
        return (pltpu.make_async_copy(wg_hbm.at[layer, ex, pl.ds(c * rows_g, rows_g), :], stage_g.at[slot], sem.at[slot, 0]),
                pltpu.make_async_copy(wu_hbm.at[layer, ex, pl.ds(c * rows_g, rows_g), :], stage_u.at[slot], sem.at[slot, 1]),
                pltpu.make_async_copy(wd_hbm.at[layer, ex, pl.ds(c * rows_d, rows_d), :], stage_d.at[slot], sem.at[slot, 2]))

    def start_chunk(ex, c, slot):
        for cp in chunk_copies(ex, c, slot):
            cp.start()

    def finish_chunk(ex, c, slot, buf):
        for cp in chunk_copies(ex, c, slot):
            cp.wait()
        wg_s[buf, pl.ds(pl.multiple_of(c * rows_g, rows_g), rows_g), :] = stage_g[slot].astype(BF16)
        wu_s[buf, pl.ds(pl.multiple_of(c * rows_g, rows_g), rows_g), :] = stage_u[slot].astype(BF16)
        wd_s[buf, pl.ds(pl.multiple_of(c * rows_d, rows_d), rows_d), :] = stage_d[slot].astype(BF16)

    @pl.when((e == 0) & (b == 0))
    def _():
        start_chunk(0, 0, 0)

        def load(c, carry):
            @pl.when(c + 1 < n_chunk)
            def _():
                start_chunk(0, c + 1, (c + 1) % 2)
            finish_chunk(0, c, c % 2, 0)
            return carry

        lax.fori_loop(0, n_chunk, load, 0)

    @pl.when(e + 1 < n_exp)
    def _():
        @pl.when(b == 0)
        def _():
            start_chunk(e + 1, 0, 0)

        @pl.when(b + 1 < n_chunk)
        def _():
            start_chunk(e + 1, b + 1, (b + 1) % 2)

    wg_ref, wu_ref, wd_ref = wg_s.at[cur], wu_s.at[cur], wd_s.at[cur]
    slot = slot_ref[...]
    aff = aff_ref[...]

    def gather(lo, hi, cap):
        n = hi - lo
        hit = lax.broadcasted_iota(I32, (cap, n), 0) == slot[:, lo:hi]
        onehot = jnp.where(hit, 1.0, 0.0).astype(BF16)
        xin = jnp.dot(onehot, h_ref[lo:hi, :], preferred_element_type=F32).astype(BF16)
        w = jnp.sum(jnp.where(hit, aff[:, lo:hi], 0.0), axis=1, keepdims=True)
        return xin, w

    xl, wl = gather(lc, h_ref.shape[0], cap_l)
    xc, wc = gather(0, lc, cap_c)
    xin = jnp.concatenate([xl, xc], axis=0)
    gate = jnp.dot(xin, wg_ref[...], preferred_element_type=F32)
    up = jnp.dot(xin, wu_ref[...], preferred_element_type=F32)
    hid = (gate * jax.nn.sigmoid(gate) * up).astype(BF16)
    y = jnp.dot(hid, wd_ref[...], preferred_element_type=F32)
    yl_ref[...] = (y[0:cap_l] * wl).astype(BF16)
    yc_ref[...] = (y[cap_l:] * wc).astype(BF16)

    @pl.when(e + 1 < n_exp)
    def _():
        finish_chunk(e + 1, b, b % 2, 1 - cur)


def _experts(hb, slot4, aff4, wg, wu, wd, layer, lc, cap_c, cap_l):
    b, t, d = hb.shape
    _, e, _, f = wg.shape
    assert d % b == 0 and f % b == 0 and (d // b) % 16 == 0
    row = pl.BlockSpec((None, None, 1, t), lambda e_, b_: (b_, e_, 0, 0))
    hbm = pl.BlockSpec(memory_space=pl.ANY)
    return pl.pallas_call(
        functools.partial(_expert_kernel, layer=layer, lc=lc, cap_c=cap_c, cap_l=cap_l),
        grid=(e, b),
        in_specs=[pl.BlockSpec((None, t, d), lambda e_, b_: (b_, 0, 0)), row, row, hbm, hbm, hbm],
        out_specs=[pl.BlockSpec((None, None, cap_l, d), lambda e_, b_: (b_, e_, 0, 0)),
                   pl.BlockSpec((None, None, cap_c, d), lambda e_, b_: (b_, e_, 0, 0))],
        out_shape=[jax.ShapeDtypeStruct((b, e, cap_l, d), BF16), jax.ShapeDtypeStruct((b, e, cap_c, d), BF16)],
        scratch_shapes=[pltpu.VMEM((2, d, f), BF16), pltpu.VMEM((2, d, f), BF16), pltpu.VMEM((2, f, d), BF16),
                        pltpu.VMEM((2, d // b, f), F32), pltpu.VMEM((2, d // b, f), F32), pltpu.VMEM((2, f // b, d), F32),
                        pltpu.SemaphoreType.DMA((2, 3))],
        compiler_params=_cparams("arbitrary", "arbitrary"),
        name="expert_ffn",
    )(hb, slot4, aff4, wg, wu, wd)


def _combine_kernel(slot_ref, yl_ref, yc_ref, x_ref, mt_ref, o_ref, *, nct, cap_c, cap_l, n_exp):
    slot = slot_ref[...]
    tm = slot.shape[0]
    gate = mt_ref[5:6, :]

    @pl.when(pl.program_id(1) < nct)
    def _():
        lanes = lax.broadcasted_iota(I32, (tm, n_exp * cap_c), 1)
        hit = jnp.zeros((tm, n_exp * cap_c), F32)
        for e in range(n_exp):
            col = slot[:, e:e + 1]
            hit = hit + jnp.where((col >= 0) & (col + e * cap_c == lanes), 1.0, 0.0)
        moe = jnp.dot(hit.astype(BF16), yc_ref[...], preferred_element_type=F32)
        o_ref[...] = x_ref[...] + gate * moe

    @pl.when(pl.program_id(1) >= nct)
    def _():
        lanes = lax.broadcasted_iota(I32, (tm, cap_l), 1)
        parts = [jnp.where(slot[:, e:e + 1] == lanes, 1.0, 0.0).astype(BF16) for e in range(n_exp)]
        moe = jnp.dot(jnp.concatenate(parts, axis=1), yl_ref[...], preferred_element_type=F32)
        o_ref[...] = x_ref[...] + gate * moe


def _combine(slot_t, yl, yc, xs, mt, tm, nct, cap_c, cap_l):
    b, t, d = xs.shape
    e = slot_t.shape[2]
    x_spec, mt_spec = _row_specs(tm, d, nct)
    return pl.pallas_call(
        functools.partial(_combine_kernel, nct=nct, cap_c=cap_c, cap_l=cap_l, n_exp=e),
        grid=(b, t // tm),
        in_specs=[pl.BlockSpec((None, tm, e), lambda b, r: (b, r, 0)),
                  pl.BlockSpec((None, e * cap_l, d), lambda b, r: (b, 0, 0)),
                  pl.BlockSpec((None, e * cap_c, d), lambda b, r: (b, 0, 0)),
                  x_spec, mt_spec],
        out_specs=x_spec,
        out_shape=jax.ShapeDtypeStruct((b, t, d), F32),
        compiler_params=_cparams("parallel", "arbitrary"),
        name="moe_combine_residual",
    )(slot_t, yl, yc, xs, mt)


def _combine_final_kernel(slot_ref, yl_ref, x_ref, mt_ref, g_ref, o_ref, *, cap_l, n_exp):
    slot = slot_ref[...]
    lanes = lax.broadcasted_iota(I32, (slot.shape[0], cap_l), 1)
    parts = [jnp.where(slot[:, e:e + 1] == lanes, 1.0, 0.0).astype(BF16) for e in range(n_exp)]
    moe = jnp.dot(jnp.concatenate(parts, axis=1), yl_ref[...], preferred_element_type=F32)
    x = x_ref[...] + mt_ref[5:6, :] * moe
    o_ref[...] = x * lax.rsqrt(jnp.mean(x * x, axis=-1, keepdims=True) + EPS) * g_ref[...]


def _combine_final(slot_t, yl, xs, mt, final_g, tm, nct, cap_l):
    b, t, d = xs.shape
    e = slot_t.shape[2]
    return pl.pallas_call(
        functools.partial(_combine_final_kernel, cap_l=cap_l, n_exp=e),
        grid=(b, t // tm - nct),
        in_specs=[pl.BlockSpec((None, tm, e), lambda b, r: (b, r + nct, 0)),
                  pl.BlockSpec((None, e * cap_l, d), lambda b, r: (b, 0, 0)),
                  pl.BlockSpec((None, tm, d), lambda b, r: (b, r + nct, 0)),
                  pl.BlockSpec((None, None, 8, d), lambda b, r: (b, 1, 0, 0)),
                  pl.BlockSpec((1, d), lambda b, r: (0, 0))],
        out_specs=pl.BlockSpec((None, tm, d), lambda b, r: (b, r, 0)),
        out_shape=jax.ShapeDtypeStruct((b, t - nct * tm, d), F32),
        compiler_params=_cparams("parallel", "arbitrary"),
        name="moe_combine_final_norm",
    )(slot_t, yl, xs, mt, final_g)


def _ec_moe(xs, hb, aff, mt, wg, wu, wd, layer, tri, tm, nct, lc, final_g=None):
    b, t, d = xs.shape
    e = aff.shape[1]
    cap_c = 2 * lc // e
    cap_l = 2 * (t - lc) // e
    slot = _topk(aff, tri, lc, cap_c, cap_l)
    yl, yc = _experts(hb, slot.reshape(b, e, 1, t), aff.reshape(b, e, 1, t), wg, wu, wd, layer, lc, cap_c, cap_l)
    if final_g is not None:
        return _combine_final(jnp.swapaxes(slot, 1, 2), yl.reshape(b, e * cap_l, d), xs, mt, final_g, tm, nct, cap_l)
    return _combine(jnp.swapaxes(slot, 1, 2), yl.reshape(b, e * cap_l, d), yc.reshape(b, e * cap_c, d),
                    xs, mt, tm, nct, cap_c, cap_l)


def _rope_tables(lc, l):
    rows = l // GRID_W
    row = jnp.repeat(jnp.arange(rows, dtype=F32), GRID_W)
    col = jnp.tile(jnp.arange(GRID_W, dtype=F32), rows)
    inv = ROPE_BASE ** (-jnp.arange(ROPE_F, dtype=F32) / ROPE_F)
    ang = jnp.stack([row[:, None] * inv, col[:, None] * inv], axis=1)
    cos, sin = jnp.cos(ang), jnp.sin(ang)
    cos_l = jnp.concatenate([cos, cos], axis=-1).reshape(l, 4 * ROPE_F)
    sin_l = jnp.concatenate([-sin, sin], axis=-1).reshape(l, 4 * ROPE_F)
    cos_l = jnp.tile(cos_l, (1, LANES // (4 * ROPE_F)))
    sin_l = jnp.tile(sin_l, (1, LANES // (4 * ROPE_F)))
    cos_t = jnp.concatenate([jnp.ones((lc, LANES), F32), cos_l], axis=0)
    sin_t = jnp.concatenate([jnp.zeros((lc, LANES), F32), sin_l], axis=0)
    return cos_t, sin_t


def kernel(x, c, ctx, c_ctx, ada_w, ada_b, norm1_g, norm2_g, final_g, attn_w_qkv, attn_w_o, attn_lam_q1, attn_lam_k1, attn_lam_q2, attn_lam_k2, attn_subln_g, ssm_lam_re, ssm_lam_im, ssm_log_dt, ssm_b_re, ssm_b_im, ssm_c_re, ssm_c_im, ssm_d, ssm_w_glu1, ssm_w_glu2, moe_w_router, moe_b_router, moe_w_gate, moe_w_up, moe_w_down):
    b, l, d = x.shape
    lc = ctx.shape[1]
    depth = ada_w.shape[0]
    t = lc + l
    tm = 256 if lc % 256 == 0 and l % 256 == 0 else 128
    assert lc % tm == 0 and l % tm == 0 and lc % S5_CHUNK == 0 and l % S5_CHUNK == 0 and d == HEADS * V_DIM
    nct = lc // tm

    xs = jnp.concatenate([ctx, x], axis=1)
    rpad = -(b + 1) % 8
    cc = jnp.concatenate([c, c_ctx[None, :], jnp.zeros((rpad, d), F32)], axis=0)
    mods = _adaln(cc, ada_w, ada_b)
    cos_t, sin_t = _rope_tables(lc, l)
    tri = (jnp.arange(l)[:, None] < jnp.arange(l)[None, :]).astype(BF16)

    for i in range(depth):
        j = i // 2
        mod_l = mods[i, :b].reshape(b, 1, N_MOD, d)
        mod_c = jnp.broadcast_to(mods[i, b].reshape(1, 1, N_MOD, d), (b, 1, N_MOD, d))
        mt = jnp.pad(jnp.concatenate([mod_c, mod_l], axis=1), ((0, 0), (0, 0), (0, 8 - N_MOD), (0, 0)))
        g1 = norm1_g[i].reshape(1, d)
        g2 = norm2_g[i].reshape(1, d)
        w_rt = moe_w_router[i].T
        b_r = moe_b_router[i].reshape(-1, 1)
        if i % 2 == 0:
            lam_init = 0.8 - 0.6 * math.exp(-0.3 * i)
            qkv = _qkv_proj(xs, mt, g1, attn_w_qkv[j].astype(BF16), cos_t, sin_t, tm, nct)
            lam_vecs = jnp.zeros((8, LANES), F32).at[0:4, 0:HEAD_DIM].set(
                jnp.stack([attn_lam_q1[j], attn_lam_k1[j], attn_lam_q2[j], attn_lam_k2[j]]))
            o = _diff_attention(qkv, lam_vecs, attn_subln_g[j].reshape(1, V_DIM), lc, lam_init)
            xs, hb, aff = _proj_res(o, attn_w_o[j].astype(BF16), xs, mt, g2, w_rt, b_r, tm, nct)
        else:
            tables = _s5_tables(ssm_lam_re[j], ssm_lam_im[j], ssm_log_dt[j], ssm_b_re[j], ssm_b_im[j],
                                ssm_c_re[j], ssm_c_im[j], ssm_d[j])
            ncc = lc // S5_CHUNK
            bb = 8 if b % 8 == 0 else b
            act = _s5_pack(xs, mt, g1, ncc, bb)
            y = _s5_groups(act, tables, b, ncc)
            xs, hb, aff = _s5_glu(y, ssm_w_glu1[j].astype(BF16), ssm_w_glu2[j].astype(BF16), xs, mt, g2, w_rt, b_r,
                                  ncc, bb)
        xs = _ec_moe(xs, hb, aff, mt, moe_w_gate, moe_w_up, moe_w_down, i,
                     tri, tm, nct, lc, final_g=final_g.reshape(1, d) if i == depth - 1 else None)
    return xs
```

```python
import functools
import math

import jax
import jax.numpy as jnp
from jax import lax
from jax.experimental import pallas as pl
from jax.experimental.pallas import tpu as pltpu

F32 = jnp.float32
BF16 = jnp.bfloat16
I32 = jnp.int32

EPS = 1e-6
N_MOD = 6
HEADS = 8
HEAD_DIM = 64
V_DIM = 2 * HEAD_DIM
ROPE_BASE = 10000.0
ROPE_F = HEAD_DIM // 4
GRID_W = 64
LANES = 128
S5_CHUNK = LANES
VMEM_LIMIT = 56 * 1024 * 1024


def _cparams(*sem):
    return pltpu.CompilerParams(dimension_semantics=sem, vmem_limit_bytes=VMEM_LIMIT)


def _split(a):
    hi = a.astype(BF16)
    lo = (a - hi.astype(F32)).astype(BF16)
    return hi, lo


def _dot3(a, b, dims):
    ah, al = _split(a)
    bh, bl = _split(b)
    dn = (dims, ((), ()))
    d = functools.partial(lax.dot_general, dimension_numbers=dn, preferred_element_type=F32)
    return d(ah, bh) + (d(ah, bl) + d(al, bh))


def _rms_mod(x, g, shift, scale):
    ms = jnp.mean(x * x, axis=-1, keepdims=True)
    return (x * lax.rsqrt(ms + EPS) * g) * (1.0 + scale) + shift


def _route(x, m, g, wr, br):
    h = _rms_mod(x, g, m[3:4, :], m[4:5, :])
    logits = _dot3(wr, h, ((1,), (1,))) + br
    p = jnp.exp(logits - jnp.max(logits, axis=0, keepdims=True))
    return h.astype(BF16), p / jnp.sum(p, axis=0, keepdims=True)


def _mod_kernel(c_ref, w_ref, b_ref, o_ref):
    c = c_ref[...]
    s = c * jax.nn.sigmoid(c)
    o_ref[...] = _dot3(s, w_ref[...], ((1,), (0,))) + b_ref[...]


def _adaln(cc, ada_w, ada_b):
    depth, d, n = ada_w.shape
    r = cc.shape[0]
    tn = n // 4
    return pl.pallas_call(
        _mod_kernel,
        grid=(depth, n // tn),
        in_specs=[pl.BlockSpec((r, d), lambda i, j: (0, 0)),
                  pl.BlockSpec((None, d, tn), lambda i, j: (i, 0, j)),
                  pl.BlockSpec((None, 1, tn), lambda i, j: (i, 0, j))],
        out_specs=pl.BlockSpec((None, r, tn), lambda i, j: (i, 0, j)),
        out_shape=jax.ShapeDtypeStruct((depth, r, n), F32),
        compiler_params=_cparams("parallel", "parallel"),
        name="adaln_mod",
    )(cc, ada_w, ada_b.reshape(depth, 1, n))


def _row_specs(tm, d, nct):
    x_spec = pl.BlockSpec((None, tm, d), lambda b, r: (b, r, 0))
    mt_spec = pl.BlockSpec((None, None, 8, d), lambda b, r: (b, jnp.where(r < nct, 0, 1), 0, 0))
    return x_spec, mt_spec


def _full(shape):
    return pl.BlockSpec(shape, lambda b, r: (0,) * len(shape))


def _qkv_kernel(x_ref, mt_ref, g_ref, w_ref, cos_ref, sin_ref, o_ref, *, d, tn):
    h = _rms_mod(x_ref[...], g_ref[...], mt_ref[0:1, :], mt_ref[1:2, :]).astype(BF16)
    reps = tn // LANES
    cosw = jnp.concatenate([cos_ref[...]] * reps, axis=1)
    sinw = jnp.concatenate([sin_ref[...]] * reps, axis=1)
    lane = lax.broadcasted_iota(I32, (1, tn), 1)
    first_half = (lane % (2 * ROPE_F)) < ROPE_F
    for j in range(3 * d // tn):
        acc = jnp.dot(h, w_ref[:, j * tn:(j + 1) * tn], preferred_element_type=F32)
        if j * tn < 2 * d:
            partner = jnp.where(first_half, pltpu.roll(acc, tn - ROPE_F, 1), pltpu.roll(acc, ROPE_F, 1))
            acc = acc * cosw + partner * sinw
        if j * tn < d:
            acc = acc * (HEAD_DIM ** -0.5 * math.log2(math.e))
        o_ref[:, j * tn:(j + 1) * tn] = acc.astype(BF16)


def _qkv_proj(xs, mt, g, w, cos_t, sin_t, tm, nct):
    b, t, d = xs.shape
    tn = 512
    x_spec, mt_spec = _row_specs(tm, d, nct)
    return pl.pallas_call(
        functools.partial(_qkv_kernel, d=d, tn=tn),
        grid=(b, t // tm),
        in_specs=[x_spec, mt_spec, _full((1, d)), _full((d, 3 * d)),
                  pl.BlockSpec((tm, LANES), lambda b, r: (r, 0)),
                  pl.BlockSpec((tm, LANES), lambda b, r: (r, 0))],
        out_specs=pl.BlockSpec((None, tm, 3 * d), lambda b, r: (b, r, 0)),
        out_shape=jax.ShapeDtypeStruct((b, t, 3 * d), BF16),
        compiler_params=_cparams("parallel", "parallel"),
        name="norm_qkv_rope",
    )(xs, mt, g, w, cos_t, sin_t)


def _attn_kernel(q_ref, k_ref, v_ref, lam_ref, g_ref, o_ref, *, lc, tq, lam_init):
    lv = lam_ref[...]
    e1 = jnp.exp(jnp.sum(lv[0:1] * lv[1:2], axis=1, keepdims=True))
    e2 = jnp.exp(jnp.sum(lv[2:3] * lv[3:4], axis=1, keepdims=True))
    lam = e1 - e2 + lam_init
    g = g_ref[...] * (1.0 - lam_init)
    lane = lax.broadcasted_iota(I32, (1, V_DIM), 1)
    comp = (lane < HEAD_DIM, lane >= HEAD_DIM)
    nt = (((1,), (1,)), ((), ()))
    t = k_ref.shape[0]
    ones_col = jnp.where(lax.broadcasted_iota(I32, (t, V_DIM), 1) == 0, 1.0, 0.0).astype(BF16)
    v1 = jnp.concatenate([v_ref[...], ones_col], axis=1)
    kk = k_ref[...]

    def block(row0, nrows, nk):
        q = q_ref[pl.ds(row0, nrows), :]
        outs = []
        for c in range(2):
            qc = jnp.where(comp[c], q, jnp.zeros_like(q))
            s = lax.dot_general(qc, kk[0:nk], nt, preferred_element_type=F32)
            p = jnp.exp2(s - jnp.max(s, axis=-1, keepdims=True))
            acc = jnp.dot(p.astype(BF16), v1[0:nk], preferred_element_type=F32)
            outs.append(acc[:, 0:V_DIM] / acc[:, V_DIM:V_DIM + 1])
        o = outs[0] - lam * outs[1]
        o = o * lax.rsqrt(jnp.mean(o * o, axis=-1, keepdims=True) + EPS) * g
        o_ref[pl.ds(row0, nrows), :] = o.astype(BF16)

    tc = min(tq, lc)
    for r in range(lc // tc):
        block(r * tc, tc, lc)

    def body(i, carry):
        block(pl.multiple_of(lc + i * tq, tq), tq, t)
        return carry

    lax.fori_loop(0, (t - lc) // tq, body, 0, unroll=8)


def _diff_attention(qkv, lam_vecs, subln_g, lc, lam_init):
    b, t, d3 = qkv.shape
    d = d3 // 3
    tq = 256 if (t - lc) % 256 == 0 and lc % 256 == 0 else 128
    slab = lambda off: pl.BlockSpec((None, t, V_DIM), lambda b, h: (b, 0, off + h))
    return pl.pallas_call(
        functools.partial(_attn_kernel, lc=lc, tq=tq, lam_init=lam_init),
        grid=(b, HEADS),
        in_specs=[slab(0), slab(HEADS), slab(2 * HEADS),
                  pl.BlockSpec((8, LANES), lambda b, h: (0, 0)),
                  pl.BlockSpec((1, V_DIM), lambda b, h: (0, 0))],
        out_specs=pl.BlockSpec((None, t, V_DIM), lambda b, h: (b, 0, h)),
        out_shape=jax.ShapeDtypeStruct((b, t, d), BF16),
        compiler_params=_cparams("parallel", "parallel"),
        name="diff_attention",
    )(qkv, qkv, qkv, lam_vecs, subln_g)


def _proj_res_kernel(a_ref, w_ref, x_ref, mt_ref, g2_ref, wr_ref, br_ref, o_ref, h_ref, aff_ref):
    acc = jnp.dot(a_ref[...], w_ref[...], preferred_element_type=F32)
    x = x_ref[...] + mt_ref[2:3, :] * acc
    o_ref[...] = x
    h_ref[...], aff_ref[...] = _route(x, mt_ref[...], g2_ref[...], wr_ref[...], br_ref[...])


def _proj_res(a, w, xs, mt, g2, w_rt, b_r, tm, nct):
    b, t, d = xs.shape
    e = w_rt.shape[0]
    x_spec, mt_spec = _row_specs(tm, d, nct)
    return pl.pallas_call(
        _proj_res_kernel,
        grid=(b, t // tm),
        in_specs=[x_spec, _full((d, d)), x_spec, mt_spec, _full((1, d)), _full((e, d)), _full((e, 1))],
        out_specs=[x_spec, x_spec, pl.BlockSpec((None, e, tm), lambda b, r: (b, 0, r))],
        out_shape=[jax.ShapeDtypeStruct((b, t, d), F32), jax.ShapeDtypeStruct((b, t, d), BF16),
                   jax.ShapeDtypeStruct((b, e, t), F32)],
        compiler_params=_cparams("parallel", "parallel"),
        name="out_proj_residual_route",
    )(a, w, xs, mt, g2, w_rt, b_r)


def _s5_pack_kernel(x_ref, mt_ref, g_ref, o_ref, *, ncc):
    is_ctx = pl.program_id(0) < ncc
    for bi in range(x_ref.shape[0]):
        m = jnp.where(is_ctx, mt_ref[bi, 0], mt_ref[bi, 1])
        h = _rms_mod(x_ref[bi], g_ref[...], m[0:1, :], m[1:2, :])
        o_ref[:, bi, :] = h.T


def _s5_pack(xs, mt, g, ncc, bb):
    b, t, d = xs.shape
    tc = S5_CHUNK
    nc, nbb = t // tc, b // bb
    return pl.pallas_call(
        functools.partial(_s5_pack_kernel, ncc=ncc),
        grid=(nc, nbb),
        in_specs=[pl.BlockSpec((bb, tc, d), lambda c, h: (h, c, 0)),
                  pl.BlockSpec((bb, 2, 8, d), lambda c, h: (h, 0, 0, 0)),
                  pl.BlockSpec((1, d), lambda c, h: (0, 0))],
        out_specs=pl.BlockSpec((d, bb, tc), lambda c, h: (0, c * nbb + h, 0)),
        out_shape=jax.ShapeDtypeStruct((d, nc * b, tc), F32),
        compiler_params=_cparams("parallel", "parallel"),
        name="s5_norm_pack",
    )(xs, mt, g)


def _s5_group_kernel(u_ref, cbf_ref, cbb_ref, pwf_ref, pwb_ref, pa_ref, pb_ref, ba_ref, bb_ref,
                     qa_ref, qb_ref, ca_ref, cb_ref, dec_ref, dsk_ref, o_ref, *, nb, nc, ncc):
    tc = S5_CHUNK
    p = u_ref.shape[0]
    u = jnp.concatenate([u_ref[i] for i in range(p)], axis=1)
    ub = u.astype(BF16)

    m_in = jnp.concatenate(
        [(pa_ref[...] * ba_ref[i:i + 1, :] + pb_ref[...] * bb_ref[i:i + 1, :]).astype(BF16) for i in range(p)], axis=0)
    s_in = jnp.dot(ub, m_in, preferred_element_type=F32)

    a1 = dec_ref[0:1, :]
    a2 = dec_ref[1:2, :]

    def step(s, c, lo):
        sl = slice(lo, lo + LANES)
        return a1[:, sl] * s + a2[:, sl] * pltpu.roll(s, LANES // 2, 1) + s_in[c * nb:(c + 1) * nb, sl]

    fwd, bwd = [None] * nc, [None] * nc
    s0 = s1 = jnp.zeros((nb, LANES), F32)
    for i in range(nc):
        c1 = ncc - 1 - i if i < ncc else nc + ncc - 1 - i
        fwd[i], bwd[c1] = s0, s1
        s0, s1 = step(s0, i, 0), step(s1, c1, LANES)
    states = jnp.concatenate([jnp.concatenate(fwd, axis=0), jnp.concatenate(bwd, axis=0)], axis=1).astype(BF16)

    m_out = jnp.concatenate(
        [(qa_ref[...] * ca_ref[:, o:o + 1] + qb_ref[...] * cb_ref[:, o:o + 1]).astype(BF16) for o in range(p)], axis=1)
    y = jnp.dot(states, m_out, preferred_element_type=F32) + dsk_ref[...] * u

    kf = _dot3(cbf_ref[...], pwf_ref[...], ((1,), (0,)))
    kb = _dot3(cbb_ref[...], pwb_ref[...], ((1,), (0,)))
    kf = kf + jnp.where(lax.broadcasted_iota(I32, kf.shape, 1) == 0, kb, 0.0)
    causal = lax.broadcasted_iota(I32, (tc, tc), 1) >= lax.broadcasted_iota(I32, (tc, tc), 0)

    def toeplitz(r):
        f = pltpu.roll(jnp.broadcast_to(kf[r:r + 1, :], (tc, tc)), 0, 1, stride=1, stride_axis=0)
        b = pltpu.roll(jnp.broadcast_to(kb[r:r + 1, :], (tc, tc)), 0, 1, stride=1, stride_axis=0)
        return jnp.where(causal, f, b).astype(BF16)

    pair = 2 * tc
    for j in range(p // 2):
        m_rows = jnp.concatenate(
            [jnp.concatenate([toeplitz(i * p + o) for o in range(p)], axis=1) for i in (2 * j, 2 * j + 1)], axis=0)
        y = y + jnp.dot(ub[:, j * pair:(j + 1) * pair], m_rows, preferred_element_type=F32)
    y = jax.nn.gelu(y)
    for o in range(p):
        o_ref[o] = y[:, o * tc:(o + 1) * tc]


def _s5_tables(lam_re, lam_im, log_dt, b_re, b_im, c_re, c_im, d_skip):
    tc = S5_CHUNK
    ndir, g, n = lam_re.shape
    p = b_re.shape[-1]
    dt = jnp.exp(log_dt)[..., None]
    mag = jnp.exp(lam_re * dt)
    ang = lam_im * dt
    ab_re, ab_im = mag * jnp.cos(ang), mag * jnp.sin(ang)
    den = lam_re * lam_re + lam_im * lam_im
    nr, ni = ab_re - 1.0, ab_im
    coef_re = (nr * lam_re + ni * lam_im) / den
    coef_im = (ni * lam_re - nr * lam_im) / den
    bb_re = coef_re[..., None] * b_re - coef_im[..., None] * b_im
    bb_im = coef_re[..., None] * b_im + coef_im[..., None] * b_re
    k = jnp.arange(tc + 1, dtype=F32)[None, None, :, None]
    pw_mag = jnp.exp(k * (lam_re * dt)[:, :, None, :])
    pw_re = pw_mag * jnp.cos(k * ang[:, :, None, :])
    pw_im = pw_mag * jnp.sin(k * ang[:, :, None, :])
    cb_re = (c_re[:, :, None, :, :] * bb_re.transpose(0, 1, 3, 2)[:, :, :, None, :]
             - c_im[:, :, None, :, :] * bb_im.transpose(0, 1, 3, 2)[:, :, :, None, :])
    cb_im = (c_re[:, :, None, :, :] * bb_im.transpose(0, 1, 3, 2)[:, :, :, None, :]
             + c_im[:, :, None, :, :] * bb_re.transpose(0, 1, 3, 2)[:, :, :, None, :])
    cb = jnp.concatenate([cb_re, -cb_im], axis=-1).reshape(ndir, g, p * p, 2 * n)
    lag_f = jnp.arange(tc)
    lag_b = jnp.concatenate([jnp.zeros((1,), jnp.int32), tc - jnp.arange(1, tc)])
    rows = lambda a: a.transpose(0, 2, 1)
    pwf = jnp.concatenate([rows(pw_re[0][:, lag_f]), rows(pw_im[0][:, lag_f])], axis=1)
    pwb = jnp.concatenate([rows(pw_re[1][:, lag_b]), rows(pw_im[1][:, lag_b])], axis=1)
    f_re, f_im = pw_re[0][:, tc - 1::-1][:, :tc], pw_im[0][:, tc - 1::-1][:, :tc]
    r_re, r_im = pw_re[1][:, :tc], pw_im[1][:, :tc]
    pa = jnp.concatenate([f_re, f_re, r_re, r_re], axis=-1)
    pb = jnp.concatenate([-f_im, f_im, -r_im, r_im], axis=-1)
    bt_re, bt_im = bb_re.transpose(0, 1, 3, 2), bb_im.transpose(0, 1, 3, 2)
    ba = jnp.concatenate([bt_re[0], bt_im[0], bt_re[1], bt_im[1]], axis=-1)
    bb = jnp.concatenate([bt_im[0], bt_re[0], bt_im[1], bt_re[1]], axis=-1)
    o_re0, o_im0 = rows(pw_re[0][:, 1:]), rows(pw_im[0][:, 1:])
    o_re1, o_im1 = rows(pw_re[1][:, tc:0:-1]), rows(pw_im[1][:, tc:0:-1])
    qa = jnp.concatenate([o_re0, -o_im0, o_re1, -o_im1], axis=1)
    qb = jnp.concatenate([-o_im0, -o_re0, -o_im1, -o_re1], axis=1)
    ct_re, ct_im = c_re.transpose(0, 1, 3, 2), c_im.transpose(0, 1, 3, 2)
    ca = jnp.concatenate([ct_re[0], ct_re[0], ct_re[1], ct_re[1]], axis=1)
    cbm = jnp.concatenate([ct_im[0], ct_im[0], ct_im[1], ct_im[1]], axis=1)
    a_re, a_im = pw_re[:, :, tc], pw_im[:, :, tc]
    a1 = jnp.concatenate([a_re[0], a_re[0], a_re[1], a_re[1]], axis=-1)
    a2 = jnp.concatenate([-a_im[0], a_im[0], -a_im[1], a_im[1]], axis=-1)
    decay = jnp.stack([a1, a2], axis=1)
    dsk = jnp.repeat(d_skip.reshape(g, 1, p), tc, axis=2)
    return cb[0], cb[1], pwf, pwb, pa, pb, ba, bb, qa, qb, ca, cbm, decay, dsk


def _s5_groups(act, tables, nb, ncc):
    d, rows, tc = act.shape
    g = tables[0].shape[0]
    p = d // g
    nc = rows // nb
    per_g = lambda a: pl.BlockSpec((None,) + a.shape[1:], lambda i: (i,) + (0,) * (a.ndim - 1))
    slab = pl.BlockSpec((p, rows, tc), lambda i: (i, 0, 0))
    sw = tables[4].shape[2]
    return pl.pallas_call(
        functools.partial(_s5_group_kernel, nb=nb, nc=nc, ncc=ncc),
        grid=(g,),
        in_specs=[slab] + [per_g(a) for a in tables],
        out_specs=slab,
        out_shape=jax.ShapeDtypeStruct((d, rows, tc), F32),
        compiler_params=_cparams("parallel"),
        name="s5_group_mix",
    )(act, *tables)


def _s5_glu_kernel(y_ref, w1_ref, w2_ref, x_ref, mt_ref, g2_ref, wr_ref, br_ref, o_ref, h_ref, aff_ref, *, ncc):
    is_ctx = pl.program_id(0) < ncc
    nbb = y_ref.shape[1]
    a = jnp.concatenate([y_ref[:, bi, :].T.astype(BF16) for bi in range(nbb)], axis=0)
    z1 = jnp.dot(a, w1_ref[...], preferred_element_type=F32)
    z2 = jnp.dot(a, w2_ref[...], preferred_element_type=F32)
    z = z1 * jax.nn.sigmoid(z2)
    tc = y_ref.shape[2]
    for bi in range(nbb):
        m = jnp.where(is_ctx, mt_ref[bi, 0], mt_ref[bi, 1])
        x = x_ref[bi] + m[2:3, :] * z[bi * tc:(bi + 1) * tc]
        o_ref[bi] = x
        h_ref[bi], aff_ref[bi] = _route(x, m, g2_ref[...], wr_ref[...], br_ref[...])


def _s5_glu(y, w1, w2, xs, mt, g2, w_rt, b_r, ncc, bb):
    b, t, d = xs.shape
    e = w_rt.shape[0]
    tc = S5_CHUNK
    nc, nbb = t // tc, b // bb
    x_spec = pl.BlockSpec((bb, tc, d), lambda c, h: (h, c, 0))
    const = lambda shape: pl.BlockSpec(shape, lambda c, h: (0,) * len(shape))
    return pl.pallas_call(
        functools.partial(_s5_glu_kernel, ncc=ncc),
        grid=(nc, nbb),
        in_specs=[pl.BlockSpec((d, bb, tc), lambda c, h: (0, c * nbb + h, 0)), const((d, d)), const((d, d)),
                  x_spec, pl.BlockSpec((bb, 2, 8, d), lambda c, h: (h, 0, 0, 0)),
                  const((1, d)), const((e, d)), const((e, 1))],
        out_specs=[x_spec, x_spec, pl.BlockSpec((bb, e, tc), lambda c, h: (h, 0, c))],
        out_shape=[jax.ShapeDtypeStruct((b, t, d), F32), jax.ShapeDtypeStruct((b, t, d), BF16),
                   jax.ShapeDtypeStruct((b, e, t), F32)],
        compiler_params=_cparams("parallel", "parallel"),
        name="s5_unpack_glu_residual_route",
    )(y, w1, w2, xs, mt, g2, w_rt, b_r)


def _kth_largest_bits(bits_sets, caps):
    e = bits_sets[0].shape[0]

    def body(_, carry):
        out = []
        for bits, cap, (lo, hi) in zip(bits_sets, caps, carry):
            mid = lo + ((hi - lo) >> 1)
            cnt = jnp.sum(jnp.where(bits >= mid, 1.0, 0.0), axis=1, keepdims=True)
            ge = cnt >= float(cap)
            out.append((jnp.where(ge, mid, lo), jnp.where(ge, hi, mid)))
        return tuple(out)

    init = tuple((jnp.zeros((e, 1), I32), jnp.full((e, 1), 0x7F800001, I32)) for _ in bits_sets)
    return [lo for lo, _ in lax.fori_loop(0, 31, body, init)]


def _slots_from_threshold(bits, thr, cap, tri):
    capf = float(cap)
    gt = bits > thr
    eq = bits == thr
    need = capf - jnp.sum(jnp.where(gt, 1.0, 0.0), axis=1, keepdims=True)
    eq_rank = jnp.dot(jnp.where(eq, 1.0, 0.0).astype(BF16), tri, preferred_element_type=F32)
    sel = jnp.where(gt, 1.0, jnp.where(eq, jnp.where(eq_rank < need, 1.0, 0.0), 0.0))
    rank = jnp.dot(sel.astype(BF16), tri, preferred_element_type=F32)
    return jnp.where(sel > 0.5, rank.astype(I32), -1)


def _topk_kernel(aff_ref, tri_ref, slot_ref, *, lc, cap_c, cap_l):
    bits_c = pltpu.bitcast(aff_ref[:, 0:lc], I32)
    bits_l = pltpu.bitcast(aff_ref[:, lc:], I32)
    thr_c, thr_l = _kth_largest_bits((bits_c, bits_l), (cap_c, cap_l))
    slot_ref[:, 0:lc] = _slots_from_threshold(bits_c, thr_c, cap_c, tri_ref[0:lc, 0:lc])
    slot_ref[:, lc:] = _slots_from_threshold(bits_l, thr_l, cap_l, tri_ref[...])


def _topk(aff, tri, lc, cap_c, cap_l):
    b, e, t = aff.shape
    l = t - lc
    return pl.pallas_call(
        functools.partial(_topk_kernel, lc=lc, cap_c=cap_c, cap_l=cap_l),
        grid=(b,),
        in_specs=[pl.BlockSpec((None, e, t), lambda i: (i, 0, 0)), pl.BlockSpec((l, l), lambda i: (0, 0))],
        out_specs=pl.BlockSpec((None, e, t), lambda i: (i, 0, 0)),
        out_shape=jax.ShapeDtypeStruct((b, e, t), I32),
        compiler_params=_cparams("parallel"),
        name="expert_choice_topk",
    )(aff, tri)


def _expert_kernel(h_ref, slot_ref, aff_ref, wg_hbm, wu_hbm, wd_hbm, yl_ref, yc_ref,
                   wg_s, wu_s, wd_s, stage_g, stage_u, stage_d, sem, *, layer, lc, cap_c, cap_l):
    e, b = pl.program_id(0), pl.program_id(1)
    n_exp, n_chunk = pl.num_programs(0), pl.num_programs(1)
    rows_g, rows_d = stage_g.shape[1], stage_d.shape[1]
    cur = e % 2

    def chunk_copies(ex, c, slot):
        return (pltpu.make_async_copy(wg_hbm.at[layer, ex, pl.ds(c * rows_g, rows_g), :], stage_g.at[slot], sem.at[slot, 0]),
                pltpu.make_async_copy(wu_hbm.at[layer, ex, pl.ds(c * rows_g, rows_g), :], stage_u.at[slot], sem.at[slot, 1]),
                pltpu.make_async_copy(wd_hbm.at[layer, ex, pl.ds(c * rows_d, rows_d), :], stage_d.at[slot], sem.at[slot, 2]))

    def start_chunk(ex, c, slot):
        for cp in chunk_copies(ex, c, slot):
            cp.start()

    def finish_chunk(ex, c, slot, buf):
        for cp in chunk_copies(ex, c, slot):
            cp.wait()
        wg_s[buf, pl.ds(pl.multiple_of(c * rows_g, rows_g), rows_g), :] = stage_g[slot].astype(BF16)
        wu_s[buf, pl.ds(pl.multiple_of(c * rows_g, rows_g), rows_g), :] = stage_u[slot].astype(BF16)
        wd_s[buf, pl.ds(pl.multiple_of(c * rows_d, rows_d), rows_d), :] = stage_d[slot].astype(BF16)

    @pl.when((e == 0) & (b == 0))
    def _():
        start_chunk(0, 0, 0)

        def load(c, carry):
            @pl.when(c + 1 < n_chunk)
            def _():
                start_chunk(0, c + 1, (c + 1) % 2)
            finish_chunk(0, c, c % 2, 0)
            return carry

        lax.fori_loop(0, n_chunk, load, 0)

    @pl.when(e + 1 < n_exp)
    def _():
        @pl.when(b == 0)
        def _():
            start_chunk(e + 1, 0, 0)

        @pl.when(b + 1 < n_chunk)
        def _():
            start_chunk(e + 1, b + 1, (b + 1) % 2)

    wg_ref, wu_ref, wd_ref = wg_s.at[cur], wu_s.at[cur], wd_s.at[cur]
    slot = slot_ref[...]
    aff = aff_ref[...]

    def gather(lo, hi, cap):
        n = hi - lo
        hit = lax.broadcasted_iota(I32, (cap, n), 0) == slot[:, lo:hi]
        onehot = jnp.where(hit, 1.0, 0.0).astype(BF16)
        xin = jnp.dot(onehot, h_ref[lo:hi, :], preferred_element_type=F32).astype(BF16)
        w = jnp.sum(jnp.where(hit, aff[:, lo:hi], 0.0), axis=1, keepdims=True)
        return xin, w

    xl, wl = gather(lc, h_ref.shape[0], cap_l)
    xc, wc = gather(0, lc, cap_c)
    xin = jnp.concatenate([xl, xc], axis=0)
    gate = jnp.dot(xin, wg_ref[...], preferred_element_type=F32)
    up = jnp.dot(xin, wu_ref[...], preferred_element_type=F32)
    hid = (gate * jax.nn.sigmoid(gate) * up).astype(BF16)
    y = jnp.dot(hid, wd_ref[...], preferred_element_type=F32)
    yl_ref[...] = (y[0:cap_l] * wl).astype(BF16)
    yc_ref[...] = (y[cap_l:] * wc).astype(BF16)

    @pl.when(e + 1 < n_exp)
    def _():
        finish_chunk(e + 1, b, b % 2, 1 - cur)


def _experts(hb, slot4, aff4, wg, wu, wd, layer, lc, cap_c, cap_l):
    b, t, d = hb.shape
    _, e, _, f = wg.shape
    assert d % b == 0 and f % b == 0 and (d // b) % 16 == 0
    row = pl.BlockSpec((None, None, 1, t), lambda e_, b_: (b_, e_, 0, 0))
    hbm = pl.BlockSpec(memory_space=pl.ANY)
    return pl.pallas_call(
        functools.partial(_expert_kernel, layer=layer, lc=lc, cap_c=cap_c, cap_l=cap_l),
        grid=(e, b),
        in_specs=[pl.BlockSpec((None, t, d), lambda e_, b_: (b_, 0, 0)), row, row, hbm, hbm, hbm],
        out_specs=[pl.BlockSpec((None, None, cap_l, d), lambda e_, b_: (b_, e_, 0, 0)),
                   pl.BlockSpec((None, None, cap_c, d), lambda e_, b_: (b_, e_, 0, 0))],
        out_shape=[jax.ShapeDtypeStruct((b, e, cap_l, d), BF16), jax.ShapeDtypeStruct((b, e, cap_c, d), BF16)],
        scratch_shapes=[pltpu.VMEM((2, d, f), BF16), pltpu.VMEM((2, d, f), BF16), pltpu.VMEM((2, f, d), BF16),
                        pltpu.VMEM((2, d // b, f), F32), pltpu.VMEM((2, d // b, f), F32), pltpu.VMEM((2, f // b, d), F32),
                        pltpu.SemaphoreType.DMA((2, 3))],
        compiler_params=_cparams("arbitrary", "arbitrary"),
        name="expert_ffn",
    )(hb, slot4, aff4, wg, wu, wd)


def _scatter_rows(slot, y, cap):
    rows = lax.broadcasted_iota(I32, (cap, slot.shape[1]), 0)
    onehot = jnp.concatenate(
        [jnp.where(slot[e:e + 1, :] == rows, 1.0, 0.0).astype(BF16) for e in range(slot.shape[0])], axis=0)
    return lax.dot_general(onehot, y, (((0,), (0,)), ((), ())), preferred_element_type=F32)


def _combine_kernel(slot_ref, yl_ref, yc_ref, x_ref, mt_ref, o_ref, *, nct, cap_c, cap_l):
    gate = mt_ref[5:6, :]

    @pl.when(pl.program_id(1) < nct)
    def _():
        o_ref[...] = x_ref[...] + gate * _scatter_rows(slot_ref[...], yc_ref[...], cap_c)

    @pl.when(pl.program_id(1) >= nct)
    def _():
        o_ref[...] = x_ref[...] + gate * _scatter_rows(slot_ref[...], yl_ref[...], cap_l)


def _combine(slot, yl, yc, xs, mt, tm, nct, cap_c, cap_l):
    b, t, d = xs.shape
    e = slot.shape[1]
    x_spec, mt_spec = _row_specs(tm, d, nct)
    return pl.pallas_call(
        functools.partial(_combine_kernel, nct=nct, cap_c=cap_c, cap_l=cap_l),
        grid=(b, t // tm),
        in_specs=[pl.BlockSpec((None, e, tm), lambda b, r: (b, 0, r)),
                  pl.BlockSpec((None, e * cap_l, d), lambda b, r: (b, 0, 0)),
                  pl.BlockSpec((None, e * cap_c, d), lambda b, r: (b, 0, 0)),
                  x_spec, mt_spec],
        out_specs=x_spec,
        out_shape=jax.ShapeDtypeStruct((b, t, d), F32),
        compiler_params=_cparams("parallel", "arbitrary"),
        name="moe_combine_residual",
    )(slot, yl, yc, xs, mt)


def _combine_final_kernel(slot_ref, yl_ref, x_ref, mt_ref, g_ref, o_ref, *, cap_l):
    x = x_ref[...] + mt_ref[5:6, :] * _scatter_rows(slot_ref[...], yl_ref[...], cap_l)
    o_ref[...] = x * lax.rsqrt(jnp.mean(x * x, axis=-1, keepdims=True) + EPS) * g_ref[...]


def _combine_final(slot, yl, xs, mt, final_g, tm, nct, cap_l):
    b, t, d = xs.shape
    e = slot.shape[1]
    return pl.pallas_call(
        functools.partial(_combine_final_kernel, cap_l=cap_l),
        grid=(b, t // tm - nct),
        in_specs=[pl.BlockSpec((None, e, tm), lambda b, r: (b, 0, r + nct)),
                  pl.BlockSpec((None, e * cap_l, d), lambda b, r: (b, 0, 0)),
                  pl.BlockSpec((None, tm, d), lambda b, r: (b, r + nct, 0)),
                  pl.BlockSpec((None, None, 8, d), lambda b, r: (b, 1, 0, 0)),
                  pl.BlockSpec((1, d), lambda b, r: (0, 0))],
        out_specs=pl.BlockSpec((None, tm, d), lambda b, r: (b, r, 0)),
        out_shape=jax.ShapeDtypeStruct((b, t - nct * tm, d), F32),
        compiler_params=_cparams("parallel", "arbitrary"),
        name="moe_combine_final_norm",
    )(slot, yl, xs, mt, final_g)


def _ec_moe(xs, hb, aff, mt, wg, wu, wd, layer, tri, tm, nct, lc, final_g=None):
    b, t, d = xs.shape
    e = aff.shape[1]
    cap_c = 2 * lc // e
    cap_l = 2 * (t - lc) // e
    slot = _topk(aff, tri, lc, cap_c, cap_l)
    yl, yc = _experts(hb, slot.reshape(b, e, 1, t), aff.reshape(b, e, 1, t), wg, wu, wd, layer, lc, cap_c, cap_l)
    if final_g is not None:
        return _combine_final(slot, yl.reshape(b, e * cap_l, d), xs, mt, final_g, tm, nct, cap_l)
    return _combine(slot, yl.reshape(b, e * cap_l, d), yc.reshape(b, e * cap_c, d), xs, mt, tm, nct, cap_c, cap_l)


def _rope_tables(lc, l):
    rows = l // GRID_W
    row = jnp.repeat(jnp.arange(rows, dtype=F32), GRID_W)
    col = jnp.tile(jnp.arange(GRID_W, dtype=F32), rows)
    inv = ROPE_BASE ** (-jnp.arange(ROPE_F, dtype=F32) / ROPE_F)
    ang = jnp.stack([row[:, None] * inv, col[:, None] * inv], axis=1)
    cos, sin = jnp.cos(ang), jnp.sin(ang)
    cos_l = jnp.concatenate([cos, cos], axis=-1).reshape(l, 4 * ROPE_F)
    sin_l = jnp.concatenate([-sin, sin], axis=-1).reshape(l, 4 * ROPE_F)
    cos_l = jnp.tile(cos_l, (1, LANES // (4 * ROPE_F)))
    sin_l = jnp.tile(sin_l, (1, LANES // (4 * ROPE_F)))
    cos_t = jnp.concatenate([jnp.ones((lc, LANES), F32), cos_l], axis=0)
    sin_t = jnp.concatenate([jnp.zeros((lc, LANES), F32), sin_l], axis=0)
    return cos_t, sin_t


def kernel(x, c, ctx, c_ctx, ada_w, ada_b, norm1_g, norm2_g, final_g, attn_w_qkv, attn_w_o, attn_lam_q1, attn_lam_k1, attn_lam_q2, attn_lam_k2, attn_subln_g, ssm_lam_re, ssm_lam_im, ssm_log_dt, ssm_b_re, ssm_b_im, ssm_c_re, ssm_c_im, ssm_d, ssm_w_glu1, ssm_w_glu2, moe_w_router, moe_b_router, moe_w_gate, moe_w_up, moe_w_down):
    b, l, d = x.shape
    lc = ctx.shape[1]
    depth = ada_w.shape[0]
    t = lc + l
    tm = 256 if lc % 256 == 0 and l % 256 == 0 else 128
    assert lc % tm == 0 and l % tm == 0 and lc % S5_CHUNK == 0 and l % S5_CHUNK == 0 and d == HEADS * V_DIM
    nct = lc // tm

    xs = jnp.concatenate([ctx, x], axis=1)
    rpad = -(b + 1) % 8
    cc = jnp.concatenate([c, c_ctx[None, :], jnp.zeros((rpad, d), F32)], axis=0)
    mods = _adaln(cc, ada_w, ada_b)
    cos_t, sin_t = _rope_tables(lc, l)
    tri = (jnp.arange(l)[:, None] < jnp.arange(l)[None, :]).astype(BF16)

    for i in range(depth):
        j = i // 2
        mod_l = mods[i, :b].reshape(b, 1, N_MOD, d)
        mod_c = jnp.broadcast_to(mods[i, b].reshape(1, 1, N_MOD, d), (b, 1, N_MOD, d))
        mt = jnp.pad(jnp.concatenate([mod_c, mod_l], axis=1), ((0, 0), (0, 0), (0, 8 - N_MOD), (0, 0)))
        g1 = norm1_g[i].reshape(1, d)
        g2 = norm2_g[i].reshape(1, d)
        w_rt = moe_w_router[i].T
        b_r = moe_b_router[i].reshape(-1, 1)
        if i % 2 == 0:
            lam_init = 0.8 - 0.6 * math.exp(-0.3 * i)
            qkv = _qkv_proj(xs, mt, g1, attn_w_qkv[j].astype(BF16), cos_t, sin_t, tm, nct)
            lam_vecs = jnp.zeros((8, LANES), F32).at[0:4, 0:HEAD_DIM].set(
                jnp.stack([attn_lam_q1[j], attn_lam_k1[j], attn_lam_q2[j], attn_lam_k2[j]]))
            o = _diff_attention(qkv, lam_vecs, attn_subln_g[j].reshape(1, V_DIM), lc, lam_init)
            xs, hb, aff = _proj_res(o, attn_w_o[j].astype(BF16), xs, mt, g2, w_rt, b_r, tm, nct)
        else:
            tables = _s5_tables(ssm_lam_re[j], ssm_lam_im[j], ssm_log_dt[j], ssm_b_re[j], ssm_b_im[j],
                                ssm_c_re[j], ssm_c_im[j], ssm_d[j])
            ncc = lc // S5_CHUNK
            bb = 8 if b % 8 == 0 else b
            act = _s5_pack(xs, mt, g1, ncc, bb)
            y = _s5_groups(act, tables, b, ncc)
            xs, hb, aff = _s5_glu(y, ssm_w_glu1[j].astype(BF16), ssm_w_glu2[j].astype(BF16), xs, mt, g2, w_rt, b_r,
                                  ncc, bb)
        xs = _ec_moe(xs, hb, aff, mt, moe_w_gate, moe_w_up, moe_w_down, i,
                     tri, tm, nct, lc, final_g=final_g.reshape(1, d) if i == depth - 1 else None)
    return xs
```

```python
import functools
import math

import jax
import jax.numpy as jnp
from jax import lax
from jax.experimental import pallas as pl
from jax.experimental.pallas import tpu as pltpu

F32 = jnp.float32
BF16 = jnp.bfloat16
I32 = jnp.int32

EPS = 1e-6
N_MOD = 6
HEADS = 8
HEAD_DIM = 64
V_DIM = 2 * HEAD_DIM
ROPE_BASE = 10000.0
ROPE_F = HEAD_DIM // 4
GRID_W = 64
LANES = 128
S5_CHUNK = LANES
VMEM_LIMIT = 56 * 1024 * 1024


def _cparams(*sem):
    return pltpu.CompilerParams(dimension_semantics=sem, vmem_limit_bytes=VMEM_LIMIT)


def _split(a):
    hi = a.astype(BF16)
    lo = (a - hi.astype(F32)).astype(BF16)
    return hi, lo


def _dot3(a, b, dims):
    ah, al = _split(a)
    bh, bl = _split(b)
    dn = (dims, ((), ()))
    d = functools.partial(lax.dot_general, dimension_numbers=dn, preferred_element_type=F32)
    return d(ah, bh) + (d(ah, bl) + d(al, bh))


def _rms_mod(x, g, shift, scale):
    ms = jnp.mean(x * x, axis=-1, keepdims=True)
    return (x * lax.rsqrt(ms + EPS) * g) * (1.0 + scale) + shift


def _route(x, m, g, wr, br):
    h = _rms_mod(x, g, m[3:4, :], m[4:5, :])
    logits = _dot3(wr, h, ((1,), (1,))) + br
    p = jnp.exp(logits - jnp.max(logits, axis=0, keepdims=True))
    return h.astype(BF16), p / jnp.sum(p, axis=0, keepdims=True)


def _mod_kernel(c_ref, w_ref, b_ref, o_ref):
    c = c_ref[...]
    s = c * jax.nn.sigmoid(c)
    o_ref[...] = _dot3(s, w_ref[...], ((1,), (0,))) + b_ref[...]


def _adaln(cc, ada_w, ada_b):
    depth, d, n = ada_w.shape
    r = cc.shape[0]
    tn = n // 4
    return pl.pallas_call(
        _mod_kernel,
        grid=(depth, n // tn),
        in_specs=[pl.BlockSpec((r, d), lambda i, j: (0, 0)),
                  pl.BlockSpec((None, d, tn), lambda i, j: (i, 0, j)),
                  pl.BlockSpec((None, 1, tn), lambda i, j: (i, 0, j))],
        out_specs=pl.BlockSpec((None, r, tn), lambda i, j: (i, 0, j)),
        out_shape=jax.ShapeDtypeStruct((depth, r, n), F32),
        compiler_params=_cparams("parallel", "parallel"),
        name="adaln_mod",
    )(cc, ada_w, ada_b.reshape(depth, 1, n))


def _row_specs(tm, d, nct):
    x_spec = pl.BlockSpec((None, tm, d), lambda b, r: (b, r, 0))
    mt_spec = pl.BlockSpec((None, None, 8, d), lambda b, r: (b, jnp.where(r < nct, 0, 1), 0, 0))
    return x_spec, mt_spec


def _full(shape):
    return pl.BlockSpec(shape, lambda b, r: (0,) * len(shape))


def _qkv_kernel(x_ref, mt_ref, g_ref, w_ref, cos_ref, sin_ref, o_ref, *, d, tn):
    h = _rms_mod(x_ref[...], g_ref[...], mt_ref[0:1, :], mt_ref[1:2, :]).astype(BF16)
    reps = tn // LANES
    cosw = jnp.concatenate([cos_ref[...]] * reps, axis=1)
    sinw = jnp.concatenate([sin_ref[...]] * reps, axis=1)
    lane = lax.broadcasted_iota(I32, (1, tn), 1)
    first_half = (lane % (2 * ROPE_F)) < ROPE_F
    for j in range(3 * d // tn):
        acc = jnp.dot(h, w_ref[:, j * tn:(j + 1) * tn], preferred_element_type=F32)
        if j * tn < 2 * d:
            partner = jnp.where(first_half, pltpu.roll(acc, tn - ROPE_F, 1), pltpu.roll(acc, ROPE_F, 1))
            acc = acc * cosw + partner * sinw
        if j * tn < d:
            acc = acc * (HEAD_DIM ** -0.5 * math.log2(math.e))
        o_ref[:, j * tn:(j + 1) * tn] = acc.astype(BF16)


def _qkv_proj(xs, mt, g, w, cos_t, sin_t, tm, nct):
    b, t, d = xs.shape
    tn = 512
    x_spec, mt_spec = _row_specs(tm, d, nct)
    return pl.pallas_call(
        functools.partial(_qkv_kernel, d=d, tn=tn),
        grid=(b, t // tm),
        in_specs=[x_spec, mt_spec, _full((1, d)), _full((d, 3 * d)),
                  pl.BlockSpec((tm, LANES), lambda b, r: (r, 0)),
                  pl.BlockSpec((tm, LANES), lambda b, r: (r, 0))],
        out_specs=pl.BlockSpec((None, tm, 3 * d), lambda b, r: (b, r, 0)),
        out_shape=jax.ShapeDtypeStruct((b, t, 3 * d), BF16),
        compiler_params=_cparams("parallel", "parallel"),
        name="norm_qkv_rope",
    )(xs, mt, g, w, cos_t, sin_t)


def _attn_kernel(q_ref, k_ref, v_ref, lam_ref, g_ref, o_ref, *, lc, tq, lam_init):
    lv = lam_ref[...]
    e1 = jnp.exp(jnp.sum(lv[0:1] * lv[1:2], axis=1, keepdims=True))
    e2 = jnp.exp(jnp.sum(lv[2:3] * lv[3:4], axis=1, keepdims=True))
    lam = e1 - e2 + lam_init
    g = g_ref[...] * (1.0 - lam_init)
    lane = lax.broadcasted_iota(I32, (1, V_DIM), 1)
    comp = (lane < HEAD_DIM, lane >= HEAD_DIM)
    nt = (((1,), (1,)), ((), ()))
    t = k_ref.shape[0]
    ones_col = jnp.where(lax.broadcasted_iota(I32, (t, V_DIM), 1) == 0, 1.0, 0.0).astype(BF16)
    v1 = jnp.concatenate([v_ref[...], ones_col], axis=1)
    kk = k_ref[...]

    def block(row0, nrows, nk):
        q = q_ref[pl.ds(row0, nrows), :]
        outs = []
        for c in range(2):
            qc = jnp.where(comp[c], q, jnp.zeros_like(q))
            s = lax.dot_general(qc, kk[0:nk], nt, preferred_element_type=F32)
            p = jnp.exp2(s - jnp.max(s, axis=-1, keepdims=True))
            acc = jnp.dot(p.astype(BF16), v1[0:nk], preferred_element_type=F32)
            outs.append(acc[:, 0:V_DIM] / acc[:, V_DIM:V_DIM + 1])
        o = outs[0] - lam * outs[1]
        o = o * lax.rsqrt(jnp.mean(o * o, axis=-1, keepdims=True) + EPS) * g
        o_ref[pl.ds(row0, nrows), :] = o.astype(BF16)

    tc = min(tq, lc)
    for r in range(lc // tc):
        block(r * tc, tc, lc)

    def body(i, carry):
        block(pl.multiple_of(lc + i * tq, tq), tq, t)
        return carry

    lax.fori_loop(0, (t - lc) // tq, body, 0, unroll=8)


def _diff_attention(qkv, lam_vecs, subln_g, lc, lam_init):
    b, t, d3 = qkv.shape
    d = d3 // 3
    tq = 256 if (t - lc) % 256 == 0 and lc % 256 == 0 else 128
    slab = lambda off: pl.BlockSpec((None, t, V_DIM), lambda b, h: (b, 0, off + h))
    return pl.pallas_call(
        functools.partial(_attn_kernel, lc=lc, tq=tq, lam_init=lam_init),
        grid=(b, HEADS),
        in_specs=[slab(0), slab(HEADS), slab(2 * HEADS),
                  pl.BlockSpec((8, LANES), lambda b, h: (0, 0)),
                  pl.BlockSpec((1, V_DIM), lambda b, h: (0, 0))],
        out_specs=pl.BlockSpec((None, t, V_DIM), lambda b, h: (b, 0, h)),
        out_shape=jax.ShapeDtypeStruct((b, t, d), BF16),
        compiler_params=_cparams("parallel", "parallel"),
        name="diff_attention",
    )(qkv, qkv, qkv, lam_vecs, subln_g)


def _proj_res_kernel(a_ref, w_ref, x_ref, mt_ref, g2_ref, wr_ref, br_ref, o_ref, h_ref, aff_ref):
    acc = jnp.dot(a_ref[...], w_ref[...], preferred_element_type=F32)
    x = x_ref[...] + mt_ref[2:3, :] * acc
    o_ref[...] = x
    h_ref[...], aff_ref[...] = _route(x, mt_ref[...], g2_ref[...], wr_ref[...], br_ref[...])


def _proj_res(a, w, xs, mt, g2, w_rt, b_r, tm, nct):
    b, t, d = xs.shape
    e = w_rt.shape[0]
    x_spec, mt_spec = _row_specs(tm, d, nct)
    return pl.pallas_call(
        _proj_res_kernel,
        grid=(b, t // tm),
        in_specs=[x_spec, _full((d, d)), x_spec, mt_spec, _full((1, d)), _full((e, d)), _full((e, 1))],
        out_specs=[x_spec, x_spec, pl.BlockSpec((None, e, tm), lambda b, r: (b, 0, r))],
        out_shape=[jax.ShapeDtypeStruct((b, t, d), F32), jax.ShapeDtypeStruct((b, t, d), BF16),
                   jax.ShapeDtypeStruct((b, e, t), F32)],
        compiler_params=_cparams("parallel", "parallel"),
        name="out_proj_residual_route",
    )(a, w, xs, mt, g2, w_rt, b_r)


def _s5_pack_kernel(x_ref, mt_ref, g_ref, o_ref, *, ncc):
    is_ctx = pl.program_id(0) < ncc
    for bi in range(x_ref.shape[0]):
        m = jnp.where(is_ctx, mt_ref[bi, 0], mt_ref[bi, 1])
        h = _rms_mod(x_ref[bi], g_ref[...], m[0:1, :], m[1:2, :])
        o_ref[:, bi, :] = h.T


def _s5_pack(xs, mt, g, ncc, bb):
    b, t, d = xs.shape
    tc = S5_CHUNK
    nc, nbb = t // tc, b // bb
    return pl.pallas_call(
        functools.partial(_s5_pack_kernel, ncc=ncc),
        grid=(nc, nbb),
        in_specs=[pl.BlockSpec((bb, tc, d), lambda c, h: (h, c, 0)),
                  pl.BlockSpec((bb, 2, 8, d), lambda c, h: (h, 0, 0, 0)),
                  pl.BlockSpec((1, d), lambda c, h: (0, 0))],
        out_specs=pl.BlockSpec((d, bb, tc), lambda c, h: (0, c * nbb + h, 0)),
        out_shape=jax.ShapeDtypeStruct((d, nc * b, tc), F32),
        compiler_params=_cparams("parallel", "parallel"),
        name="s5_norm_pack",
    )(xs, mt, g)


def _s5_group_kernel(u_ref, cbf_ref, cbb_ref, pwf_ref, pwb_ref, pa_ref, pb_ref, ba_ref, bb_ref,
                     qa_ref, qb_ref, ca_ref, cb_ref, dec_ref, dsk_ref, o_ref, *, nb, nc, ncc):
    tc = S5_CHUNK
    p = u_ref.shape[0]
    u = jnp.concatenate([u_ref[i] for i in range(p)], axis=1)
    ub = u.astype(BF16)

    m_in = jnp.concatenate(
        [(pa_ref[...] * ba_ref[i:i + 1, :] + pb_ref[...] * bb_ref[i:i + 1, :]).astype(BF16) for i in range(p)], axis=0)
    s_in = jnp.dot(ub, m_in, preferred_element_type=F32)

    a1 = dec_ref[0:1, :]
    a2 = dec_ref[1:2, :]

    def step(s, c, lo):
        sl = slice(lo, lo + LANES)
        return a1[:, sl] * s + a2[:, sl] * pltpu.roll(s, LANES // 2, 1) + s_in[c * nb:(c + 1) * nb, sl]

    fwd, bwd = [None] * nc, [None] * nc
    s0 = s1 = jnp.zeros((nb, LANES), F32)
    for i in range(nc):
        c1 = ncc - 1 - i if i < ncc else nc + ncc - 1 - i
        fwd[i], bwd[c1] = s0, s1
        s0, s1 = step(s0, i, 0), step(s1, c1, LANES)
    states = jnp.concatenate([jnp.concatenate(fwd, axis=0), jnp.concatenate(bwd, axis=0)], axis=1).astype(BF16)

    m_out = jnp.concatenate(
        [(qa_ref[...] * ca_ref[:, o:o + 1] + qb_ref[...] * cb_ref[:, o:o + 1]).astype(BF16) for o in range(p)], axis=1)
    y = jnp.dot(states, m_out, preferred_element_type=F32) + dsk_ref[...] * u

    kf = _dot3(cbf_ref[...], pwf_ref[...], ((1,), (0,)))
    kb = _dot3(cbb_ref[...], pwb_ref[...], ((1,), (0,)))
    kf = kf + jnp.where(lax.broadcasted_iota(I32, kf.shape, 1) == 0, kb, 0.0)
    causal = lax.broadcasted_iota(I32, (tc, tc), 1) >= lax.broadcasted_iota(I32, (tc, tc), 0)

    def toeplitz(r):
        f = pltpu.roll(jnp.broadcast_to(kf[r:r + 1, :], (tc, tc)), 0, 1, stride=1, stride_axis=0)
        b = pltpu.roll(jnp.broadcast_to(kb[r:r + 1, :], (tc, tc)), 0, 1, stride=1, stride_axis=0)
        return jnp.where(causal, f, b).astype(BF16)

    pair = 2 * tc
    for j in range(p // 2):
        m_rows = jnp.concatenate(
            [jnp.concatenate([toeplitz(i * p + o) for o in range(p)], axis=1) for i in (2 * j, 2 * j + 1)], axis=0)
        y = y + jnp.dot(ub[:, j * pair:(j + 1) * pair], m_rows, preferred_element_type=F32)
    y = jax.nn.gelu(y)
    for o in range(p):
        o_ref[o] = y[:, o * tc:(o + 1) * tc]


def _s5_tables(lam_re, lam_im, log_dt, b_re, b_im, c_re, c_im, d_skip):
    tc = S5_CHUNK
    ndir, g, n = lam_re.shape
    p = b_re.shape[-1]
    dt = jnp.exp(log_dt)[..., None]
    mag = jnp.exp(lam_re * dt)
    ang = lam_im * dt
    ab_re, ab_im = mag * jnp.cos(ang), mag * jnp.sin(ang)
    den = lam_re * lam_re + lam_im * lam_im
    nr, ni = ab_re - 1.0, ab_im
    coef_re = (nr * lam_re + ni * lam_im) / den
    coef_im = (ni * lam_re - nr * lam_im) / den
    bb_re = coef_re[..., None] * b_re - coef_im[..., None] * b_im
    bb_im = coef_re[..., None] * b_im + coef_im[..., None] * b_re
    k = jnp.arange(tc + 1, dtype=F32)[None, None, :, None]
    pw_mag = jnp.exp(k * (lam_re * dt)[:, :, None, :])
    pw_re = pw_mag * jnp.cos(k * ang[:, :, None, :])
    pw_im = pw_mag * jnp.sin(k * ang[:, :, None, :])
    cb_re = (c_re[:, :, None, :, :] * bb_re.transpose(0, 1, 3, 2)[:, :, :, None, :]
             - c_im[:, :, None, :, :] * bb_im.transpose(0, 1, 3, 2)[:, :, :, None, :])
    cb_im = (c_re[:, :, None, :, :] * bb_im.transpose(0, 1, 3, 2)[:, :, :, None, :]
             + c_im[:, :, None, :, :] * bb_re.transpose(0, 1, 3, 2)[:, :, :, None, :])
    cb = jnp.concatenate([cb_re, -cb_im], axis=-1).reshape(ndir, g, p * p, 2 * n)
    lag_f = jnp.arange(tc)
    lag_b = jnp.concatenate([jnp.zeros((1,), jnp.int32), tc - jnp.arange(1, tc)])
    rows = lambda a: a.transpose(0, 2, 1)
    pwf = jnp.concatenate([rows(pw_re[0][:, lag_f]), rows(pw_im[0][:, lag_f])], axis=1)
    pwb = jnp.concatenate([rows(pw_re[1][:, lag_b]), rows(pw_im[1][:, lag_b])], axis=1)
    f_re, f_im = pw_re[0][:, tc - 1::-1][:, :tc], pw_im[0][:, tc - 1::-1][:, :tc]
    r_re, r_im = pw_re[1][:, :tc], pw_im[1][:, :tc]
    pa = jnp.concatenate([f_re, f_re, r_re, r_re], axis=-1)
    pb = jnp.concatenate([-f_im, f_im, -r_im, r_im], axis=-1)
    bt_re, bt_im = bb_re.transpose(0, 1, 3, 2), bb_im.transpose(0, 1, 3, 2)
    ba = jnp.concatenate([bt_re[0], bt_im[0], bt_re[1], bt_im[1]], axis=-1)
    bb = jnp.concatenate([bt_im[0], bt_re[0], bt_im[1], bt_re[1]], axis=-1)
    o_re0, o_im0 = rows(pw_re[0][:, 1:]), rows(pw_im[0][:, 1:])
    o_re1, o_im1 = rows(pw_re[1][:, tc:0:-1]), rows(pw_im[1][:, tc:0:-1])
    qa = jnp.concatenate([o_re0, -o_im0, o_re1, -o_im1], axis=1)
    qb = jnp.concatenate([-o_im0, -o_re0, -o_im1, -o_re1], axis=1)
    ct_re, ct_im = c_re.transpose(0, 1, 3, 2), c_im.transpose(0, 1, 3, 2)
    ca = jnp.concatenate([ct_re[0], ct_re[0], ct_re[1], ct_re[1]], axis=1)
    cbm = jnp.concatenate([ct_im[0], ct_im[0], ct_im[1], ct_im[1]], axis=1)
    a_re, a_im = pw_re[:, :, tc], pw_im[:, :, tc]
    a1 = jnp.concatenate([a_re[0], a_re[0], a_re[1], a_re[1]], axis=-1)
    a2 = jnp.concatenate([-a_im[0], a_im[0], -a_im[1], a_im[1]], axis=-1)
    decay = jnp.stack([a1, a2], axis=1)
    dsk = jnp.repeat(d_skip.reshape(g, 1, p), tc, axis=2)
    return cb[0], cb[1], pwf, pwb, pa, pb, ba, bb, qa, qb, ca, cbm, decay, dsk


def _s5_groups(act, tables, nb, ncc):
    d, rows, tc = act.shape
    g = tables[0].shape[0]
    p = d // g
    nc = rows // nb
    per_g = lambda a: pl.BlockSpec((None,) + a.shape[1:], lambda i: (i,) + (0,) * (a.ndim - 1))
    slab = pl.BlockSpec((p, rows, tc), lambda i: (i, 0, 0))
    sw = tables[4].shape[2]
    return pl.pallas_call(
        functools.partial(_s5_group_kernel, nb=nb, nc=nc, ncc=ncc),
        grid=(g,),
        in_specs=[slab] + [per_g(a) for a in tables],
        out_specs=slab,
        out_shape=jax.ShapeDtypeStruct((d, rows, tc), F32),
        compiler_params=_cparams("parallel"),
        name="s5_group_mix",
    )(act, *tables)


def _s5_glu_kernel(y_ref, w1_ref, w2_ref, x_ref, mt_ref, g2_ref, wr_ref, br_ref, o_ref, h_ref, aff_ref, *, ncc):
    is_ctx = pl.program_id(0) < ncc
    nbb = y_ref.shape[1]
    a = jnp.concatenate([y_ref[:, bi, :].T.astype(BF16) for bi in range(nbb)], axis=0)
    z1 = jnp.dot(a, w1_ref[...], preferred_element_type=F32)
    z2 = jnp.dot(a, w2_ref[...], preferred_element_type=F32)
    z = z1 * jax.nn.sigmoid(z2)
    tc = y_ref.shape[2]
    for bi in range(nbb):
        m = jnp.where(is_ctx, mt_ref[bi, 0], mt_ref[bi, 1])
        x = x_ref[bi] + m[2:3, :] * z[bi * tc:(bi + 1) * tc]
        o_ref[bi] = x
        h_ref[bi], aff_ref[bi] = _route(x, m, g2_ref[...], wr_ref[...], br_ref[...])


def _s5_glu(y, w1, w2, xs, mt, g2, w_rt, b_r, ncc, bb):
    b, t, d = xs.shape
    e = w_rt.shape[0]
    tc = S5_CHUNK
    nc, nbb = t // tc, b // bb
    x_spec = pl.BlockSpec((bb, tc, d), lambda c, h: (h, c, 0))
    const = lambda shape: pl.BlockSpec(shape, lambda c, h: (0,) * len(shape))
    return pl.pallas_call(
        functools.partial(_s5_glu_kernel, ncc=ncc),
        grid=(nc, nbb),
        in_specs=[pl.BlockSpec((d, bb, tc), lambda c, h: (0, c * nbb + h, 0)), const((d, d)), const((d, d)),
                  x_spec, pl.BlockSpec((bb, 2, 8, d), lambda c, h: (h, 0, 0, 0)),
                  const((1, d)), const((e, d)), const((e, 1))],
        out_specs=[x_spec, x_spec, pl.BlockSpec((bb, e, tc), lambda c, h: (h, 0, c))],
        out_shape=[jax.ShapeDtypeStruct((b, t, d), F32), jax.ShapeDtypeStruct((b, t, d), BF16),
                   jax.ShapeDtypeStruct((b, e, t), F32)],
        compiler_params=_cparams("parallel", "parallel"),
        name="s5_unpack_glu_residual_route",
    )(y, w1, w2, xs, mt, g2, w_rt, b_r)


def _kth_largest_bits(bits_sets, caps):
    e = bits_sets[0].shape[0]

    def body(_, carry):
        out = []
        for bits, cap, (lo, hi) in zip(bits_sets, caps, carry):
            mid = lo + ((hi - lo) >> 1)
            cnt = jnp.sum(jnp.where(bits >= mid, 1.0, 0.0), axis=1, keepdims=True)
            ge = cnt >= float(cap)
            out.append((jnp.where(ge, mid, lo), jnp.where(ge, hi, mid)))
        return tuple(out)

    init = tuple((jnp.zeros((e, 1), I32), jnp.full((e, 1), 0x7F800001, I32)) for _ in bits_sets)
    return [lo for lo, _ in lax.fori_loop(0, 31, body, init)]


def _slots_from_threshold(bits, thr, cap, tri):
    capf = float(cap)
    gt = bits > thr
    eq = bits == thr
    need = capf - jnp.sum(jnp.where(gt, 1.0, 0.0), axis=1, keepdims=True)
    eq_rank = jnp.dot(jnp.where(eq, 1.0, 0.0).astype(BF16), tri, preferred_element_type=F32)
    sel = jnp.where(gt, 1.0, jnp.where(eq, jnp.where(eq_rank < need, 1.0, 0.0), 0.0))
    rank = jnp.dot(sel.astype(BF16), tri, preferred_element_type=F32)
    return jnp.where(sel > 0.5, rank.astype(I32), -1)


def _topk_kernel(aff_ref, tri_ref, slot_ref, *, lc, cap_c, cap_l):
    bits_c = pltpu.bitcast(aff_ref[:, 0:lc], I32)
    bits_l = pltpu.bitcast(aff_ref[:, lc:], I32)
    thr_c, thr_l = _kth_largest_bits((bits_c, bits_l), (cap_c, cap_l))
    slot_ref[:, 0:lc] = _slots_from_threshold(bits_c, thr_c, cap_c, tri_ref[0:lc, 0:lc])
    slot_ref[:, lc:] = _slots_from_threshold(bits_l, thr_l, cap_l, tri_ref[...])


def _topk(aff, tri, lc, cap_c, cap_l):
    b, e, t = aff.shape
    l = t - lc
    return pl.pallas_call(
        functools.partial(_topk_kernel, lc=lc, cap_c=cap_c, cap_l=cap_l),
        grid=(b,),
        in_specs=[pl.BlockSpec((None, e, t), lambda i: (i, 0, 0)), pl.BlockSpec((l, l), lambda i: (0, 0))],
        out_specs=pl.BlockSpec((None, e, t), lambda i: (i, 0, 0)),
        out_shape=jax.ShapeDtypeStruct((b, e, t), I32),
        compiler_params=_cparams("parallel"),
        name="expert_choice_topk",
    )(aff, tri)


def _expert_kernel(h_ref, slot_ref, aff_ref, wg_hbm, wu_hbm, wd_hbm, *refs, layer, lc, cap_c, cap_l, with_ctx):
    n_out = 2 if with_ctx else 1
    out_refs = refs[:n_out]
    wg_s, wu_s, wd_s, stage_g, stage_u, stage_d, sem = refs[n_out:]
    e, b = pl.program_id(0), pl.program_id(1)
    n_exp, n_chunk = pl.num_programs(0), pl.num_programs(1)
    rows_g, rows_d = stage_g.shape[1], stage_d.shape[1]
    cur = e % 2

    def chunk_copies(ex, c, slot):
        return (pltpu.make_async_copy(wg_hbm.at[layer, ex, pl.ds(c * rows_g, rows_g), :], stage_g.at[slot], sem.at[slot, 0]),
                pltpu.make_async_copy(wu_hbm.at[layer, ex, pl.ds(c * rows_g, rows_g), :], stage_u.at[slot], sem.at[slot, 1]),
                pltpu.make_async_copy(wd_hbm.at[layer, ex, pl.ds(c * rows_d, rows_d), :], stage_d.at[slot], sem.at[slot, 2]))

    def start_chunk(ex, c, slot):
        for cp in chunk_copies(ex, c, slot):
            cp.start()

    def finish_chunk(ex, c, slot, buf):
        for cp in chunk_copies(ex, c, slot):
            cp.wait()
        wg_s[buf, pl.ds(pl.multiple_of(c * rows_g, rows_g), rows_g), :] = stage_g[slot].astype(BF16)
        wu_s[buf, pl.ds(pl.multiple_of(c * rows_g, rows_g), rows_g), :] = stage_u[slot].astype(BF16)
        wd_s[buf, pl.ds(pl.multiple_of(c * rows_d, rows_d), rows_d), :] = stage_d[slot].astype(BF16)

    @pl.when((e == 0) & (b == 0))
    def _():
        start_chunk(0, 0, 0)

        def load(c, carry):
            @pl.when(c + 1 < n_chunk)
            def _():
                start_chunk(0, c + 1, (c + 1) % 2)
            finish_chunk(0, c, c % 2, 0)
            return carry

        lax.fori_loop(0, n_chunk, load, 0)

    @pl.when(e + 1 < n_exp)
    def _():
        @pl.when(b == 0)
        def _():
            start_chunk(e + 1, 0, 0)

        @pl.when(b + 1 < n_chunk)
        def _():
            start_chunk(e + 1, b + 1, (b + 1) % 2)

    wg_ref, wu_ref, wd_ref = wg_s.at[cur], wu_s.at[cur], wd_s.at[cur]
    slot = slot_ref[...]
    aff = aff_ref[...]

    def gather(lo, hi, cap):
        n = hi - lo
        hit = lax.broadcasted_iota(I32, (cap, n), 0) == slot[:, lo:hi]
        onehot = jnp.where(hit, 1.0, 0.0).astype(BF16)
        xin = jnp.dot(onehot, h_ref[lo:hi, :], preferred_element_type=F32).astype(BF16)
        w = jnp.sum(jnp.where(hit, aff[:, lo:hi], 0.0), axis=1, keepdims=True)
        return xin, w

    xin, wl = gather(lc, h_ref.shape[0], cap_l)
    if with_ctx:
        xc, wc = gather(0, lc, cap_c)
        xin = jnp.concatenate([xin, xc], axis=0)
    gate = jnp.dot(xin, wg_ref[...], preferred_element_type=F32)
    up = jnp.dot(xin, wu_ref[...], preferred_element_type=F32)
    hid = (gate * jax.nn.sigmoid(gate) * up).astype(BF16)
    y = jnp.dot(hid, wd_ref[...], preferred_element_type=F32)
    out_refs[0][...] = (y[0:cap_l] * wl).astype(BF16)
    if with_ctx:
        out_refs[1][...] = (y[cap_l:] * wc).astype(BF16)

    @pl.when(e + 1 < n_exp)
    def _():
        finish_chunk(e + 1, b, b % 2, 1 - cur)


def _experts(hb, slot4, aff4, wg, wu, wd, layer, lc, cap_c, cap_l, with_ctx):
    b, t, d = hb.shape
    _, e, _, f = wg.shape
    assert d % b == 0 and f % b == 0 and (d // b) % 16 == 0
    row = pl.BlockSpec((None, None, 1, t), lambda e_, b_: (b_, e_, 0, 0))
    hbm = pl.BlockSpec(memory_space=pl.ANY)
    caps = (cap_l, cap_c) if with_ctx else (cap_l,)
    return pl.pallas_call(
        functools.partial(_expert_kernel, layer=layer, lc=lc, cap_c=cap_c, cap_l=cap_l, with_ctx=with_ctx),
        grid=(e, b),
        in_specs=[pl.BlockSpec((None, t, d), lambda e_, b_: (b_, 0, 0)), row, row, hbm, hbm, hbm],
        out_specs=[pl.BlockSpec((None, None, cap, d), lambda e_, b_: (b_, e_, 0, 0)) for cap in caps],
        out_shape=[jax.ShapeDtypeStruct((b, e, cap, d), BF16) for cap in caps],
        scratch_shapes=[pltpu.VMEM((2, d, f), BF16), pltpu.VMEM((2, d, f), BF16), pltpu.VMEM((2, f, d), BF16),
                        pltpu.VMEM((2, d // b, f), F32), pltpu.VMEM((2, d // b, f), F32), pltpu.VMEM((2, f // b, d), F32),
                        pltpu.SemaphoreType.DMA((2, 3))],
        compiler_params=_cparams("arbitrary", "arbitrary"),
        name="expert_ffn",
    )(hb, slot4, aff4, wg, wu, wd)


def _scatter_rows(slot, y, cap):
    rows = lax.broadcasted_iota(I32, (cap, slot.shape[1]), 0)
    onehot = jnp.concatenate(
        [jnp.where(slot[e:e + 1, :] == rows, 1.0, 0.0).astype(BF16) for e in range(slot.shape[0])], axis=0)
    return lax.dot_general(onehot, y, (((0,), (0,)), ((), ())), preferred_element_type=F32)


def _combine_kernel(slot_ref, yl_ref, yc_ref, x_ref, mt_ref, o_ref, *, nct, cap_c, cap_l):
    gate = mt_ref[5:6, :]

    @pl.when(pl.program_id(1) < nct)
    def _():
        o_ref[...] = x_ref[...] + gate * _scatter_rows(slot_ref[...], yc_ref[...], cap_c)

    @pl.when(pl.program_id(1) >= nct)
    def _():
        o_ref[...] = x_ref[...] + gate * _scatter_rows(slot_ref[...], yl_ref[...], cap_l)


def _combine(slot, yl, yc, xs, mt, tm, nct, cap_c, cap_l):
    b, t, d = xs.shape
    e = slot.shape[1]
    x_spec, mt_spec = _row_specs(tm, d, nct)
    return pl.pallas_call(
        functools.partial(_combine_kernel, nct=nct, cap_c=cap_c, cap_l=cap_l),
        grid=(b, t // tm),
        in_specs=[pl.BlockSpec((None, e, tm), lambda b, r: (b, 0, r)),
                  pl.BlockSpec((None, e * cap_l, d), lambda b, r: (b, 0, 0)),
                  pl.BlockSpec((None, e * cap_c, d), lambda b, r: (b, 0, 0)),
                  x_spec, mt_spec],
        out_specs=x_spec,
        out_shape=jax.ShapeDtypeStruct((b, t, d), F32),
        compiler_params=_cparams("parallel", "arbitrary"),
        name="moe_combine_residual",
    )(slot, yl, yc, xs, mt)


def _combine_final_kernel(slot_ref, yl_ref, x_ref, mt_ref, g_ref, o_ref, *, cap_l):
    x = x_ref[...] + mt_ref[5:6, :] * _scatter_rows(slot_ref[...], yl_ref[...], cap_l)
    o_ref[...] = x * lax.rsqrt(jnp.mean(x * x, axis=-1, keepdims=True) + EPS) * g_ref[...]


def _combine_final(slot, yl, xs, mt, final_g, tm, nct, cap_l):
    b, t, d = xs.shape
    e = slot.shape[1]
    return pl.pallas_call(
        functools.partial(_combine_final_kernel, cap_l=cap_l),
        grid=(b, t // tm - nct),
        in_specs=[pl.BlockSpec((None, e, tm), lambda b, r: (b, 0, r + nct)),
                  pl.BlockSpec((None, e * cap_l, d), lambda b, r: (b, 0, 0)),
                  pl.BlockSpec((None, tm, d), lambda b, r: (b, r + nct, 0)),
                  pl.BlockSpec((None, None, 8, d), lambda b, r: (b, 1, 0, 0)),
                  pl.BlockSpec((1, d), lambda b, r: (0, 0))],
        out_specs=pl.BlockSpec((None, tm, d), lambda b, r: (b, r, 0)),
        out_shape=jax.ShapeDtypeStruct((b, t - nct * tm, d), F32),
        compiler_params=_cparams("parallel", "arbitrary"),
        name="moe_combine_final_norm",
    )(slot, yl, xs, mt, final_g)


def _ec_moe(xs, hb, aff, mt, wg, wu, wd, layer, tri, tm, nct, lc, final_g=None):
    b, t, d = xs.shape
    e = aff.shape[1]
    cap_c = 2 * lc // e
    cap_l = 2 * (t - lc) // e
    slot = _topk(aff, tri, lc, cap_c, cap_l)
    last = final_g is not None
    ys = _experts(hb, slot.reshape(b, e, 1, t), aff.reshape(b, e, 1, t), wg, wu, wd, layer, lc, cap_c, cap_l,
                  with_ctx=not last)
    yl = ys[0].reshape(b, e * cap_l, d)
    if last:
        return _combine_final(slot, yl, xs, mt, final_g, tm, nct, cap_l)
    return _combine(slot, yl, ys[1].reshape(b, e * cap_c, d), xs, mt, tm, nct, cap_c, cap_l)


def _rope_tables(lc, l):
    rows = l // GRID_W
    row = jnp.repeat(jnp.arange(rows, dtype=F32), GRID_W)
    col = jnp.tile(jnp.arange(GRID_W, dtype=F32), rows)
    inv = ROPE_BASE ** (-jnp.arange(ROPE_F, dtype=F32) / ROPE_F)
    ang = jnp.stack([row[:, None] * inv, col[:, None] * inv], axis=1)
    cos, sin = jnp.cos(ang), jnp.sin(ang)
    cos_l = jnp.concatenate([cos, cos], axis=-1).reshape(l, 4 * ROPE_F)
    sin_l = jnp.concatenate([-sin, sin], axis=-1).reshape(l, 4 * ROPE_F)
    cos_l = jnp.tile(cos_l, (1, LANES // (4 * ROPE_F)))
    sin_l = jnp.tile(sin_l, (1, LANES // (4 * ROPE_F)))
    cos_t = jnp.concatenate([jnp.ones((lc, LANES), F32), cos_l], axis=0)
    sin_t = jnp.concatenate([jnp.zeros((lc, LANES), F32), sin_l], axis=0)
    return cos_t, sin_t


def kernel(x, c, ctx, c_ctx, ada_w, ada_b, norm1_g, norm2_g, final_g, attn_w_qkv, attn_w_o, attn_lam_q1, attn_lam_k1, attn_lam_q2, attn_lam_k2, attn_subln_g, ssm_lam_re, ssm_lam_im, ssm_log_dt, ssm_b_re, ssm_b_im, ssm_c_re, ssm_c_im, ssm_d, ssm_w_glu1, ssm_w_glu2, moe_w_router, moe_b_router, moe_w_gate, moe_w_up, moe_w_down):
    b, l, d = x.shape
    lc = ctx.shape[1]
    depth = ada_w.shape[0]
    t = lc + l
    tm = 256 if lc % 256 == 0 and l % 256 == 0 else 128
    assert lc % tm == 0 and l % tm == 0 and lc % S5_CHUNK == 0 and l % S5_CHUNK == 0 and d == HEADS * V_DIM
    nct = lc // tm

    xs = jnp.concatenate([ctx, x], axis=1)
    rpad = -(b + 1) % 8
    cc = jnp.concatenate([c, c_ctx[None, :], jnp.zeros((rpad, d), F32)], axis=0)
    mods = _adaln(cc, ada_w, ada_b)
    cos_t, sin_t = _rope_tables(lc, l)
    tri = (jnp.arange(l)[:, None] < jnp.arange(l)[None, :]).astype(BF16)

    for i in range(depth):
        j = i // 2
        mod_l = mods[i, :b].reshape(b, 1, N_MOD, d)
        mod_c = jnp.broadcast_to(mods[i, b].reshape(1, 1, N_MOD, d), (b, 1, N_MOD, d))
        mt = jnp.pad(jnp.concatenate([mod_c, mod_l], axis=1), ((0, 0), (0, 0), (0, 8 - N_MOD), (0, 0)))
        g1 = norm1_g[i].reshape(1, d)
        g2 = norm2_g[i].reshape(1, d)
        w_rt = moe_w_router[i].T
        b_r = moe_b_router[i].reshape(-1, 1)
        if i % 2 == 0:
            lam_init = 0.8 - 0.6 * math.exp(-0.3 * i)
            qkv = _qkv_proj(xs, mt, g1, attn_w_qkv[j].astype(BF16), cos_t, sin_t, tm, nct)
            lam_vecs = jnp.zeros((8, LANES), F32).at[0:4, 0:HEAD_DIM].set(
                jnp.stack([attn_lam_q1[j], attn_lam_k1[j], attn_lam_q2[j], attn_lam_k2[j]]))
            o = _diff_attention(qkv, lam_vecs, attn_subln_g[j].reshape(1, V_DIM), lc, lam_init)
            xs, hb, aff = _proj_res(o, attn_w_o[j].astype(BF16), xs, mt, g2, w_rt, b_r, tm, nct)
        else:
            tables = _s5_tables(ssm_lam_re[j], ssm_lam_im[j], ssm_log_dt[j], ssm_b_re[j], ssm_b_im[j],
                                ssm_c_re[j], ssm_c_im[j], ssm_d[j])
            ncc = lc // S5_CHUNK
            bb = 8 if b % 8 == 0 else b
            act = _s5_pack(xs, mt, g1, ncc, bb)
            y = _s5_groups(act, tables, b, ncc)
            xs, hb, aff = _s5_glu(y, ssm_w_glu1[j].astype(BF16), ssm_w_glu2[j].astype(BF16), xs, mt, g2, w_rt, b_r,
                                  ncc, bb)
        xs = _ec_moe(xs, hb, aff, mt, moe_w_gate, moe_w_up, moe_w_down, i,
                     tri, tm, nct, lc, final_g=final_g.reshape(1, d) if i == depth - 1 else None)
    return xs
```
